```python
import math
import jax, jax.numpy as jnp
from jax import lax
import numpy as np

D_MODEL = 2048
BATCH = 16
SEQ = 2048
DEPTH = 1
DEC_BATCH = 4
DEC_SEQ = 4096
PAST_LEN = 128

MIX_WIDTH = D_MODEL
CONV_WIDTH = D_MODEL // 2
RET_WIDTH = MIX_WIDTH - CONV_WIDTH
RET_HEADS = 8
RET_HEAD_DIM = RET_WIDTH // RET_HEADS
CONV_KERNEL = 31
CONV_PAD = CONV_KERNEL // 2
CHUNK = 128
D_FF = -(-8 * D_MODEL // (3 * 256)) * 256
ROPE_BASE = 10000.0
EPS = 1e-6
IN_COLS = 2 * CONV_WIDTH + 4 * RET_WIDTH

kernel_name = "hymba_conformer_retnet_encoder"


def rms_norm(x, w):
    xf = x.astype(jnp.float32)
    y = xf * lax.rsqrt(jnp.mean(xf * xf, axis=-1, keepdims=True) + EPS)
    return (y * w.astype(jnp.float32)).astype(x.dtype)


def layer_norm(x, w, b):
    xf = x.astype(jnp.float32)
    mu = jnp.mean(xf, axis=-1, keepdims=True)
    var = jnp.mean(jnp.square(xf - mu), axis=-1, keepdims=True)
    y = (xf - mu) * lax.rsqrt(var + EPS)
    return (y * w.astype(jnp.float32) + b.astype(jnp.float32)).astype(x.dtype)


def rotary(x):
    S, d = x.shape[2], x.shape[3]
    inv_freq = ROPE_BASE ** (-jnp.arange(0, d, 2, dtype=jnp.float32) / d)
    ang = jnp.arange(S, dtype=jnp.float32)[:, None] * inv_freq[None, :]
    cos, sin = jnp.cos(ang), jnp.sin(ang)
    x1, x2 = x[..., : d // 2], x[..., d // 2:]
    return jnp.concatenate([x1 * cos - x2 * sin, x1 * sin + x2 * cos], axis=-1)


def conv_mixer(u, conv_w, conv_b, ln_w, ln_b):
    a, g = u[..., :CONV_WIDTH], u[..., CONV_WIDTH:]
    h = a * jax.nn.sigmoid(g)
    h = lax.conv_general_dilated(
        h, conv_w.reshape(CONV_KERNEL, 1, CONV_WIDTH).astype(h.dtype),
        window_strides=(1,), padding=[(CONV_PAD, CONV_PAD)],
        dimension_numbers=("NWC", "WIO", "NWC"),
        feature_group_count=CONV_WIDTH) + conv_b.astype(h.dtype)
    h = layer_norm(h, ln_w, ln_b)
    return jax.nn.silu(h)


def retention_direction(q, k, v, log_g, strict):
    B, H, S, d = q.shape
    N = S // CHUNK
    qc = q.reshape(B, H, N, CHUNK, d)
    kc = k.reshape(B, H, N, CHUNK, d)
    vc = v.reshape(B, H, N, CHUNK, d)
    idx = jnp.arange(CHUNK, dtype=jnp.float32)
    diff = idx[:, None] - idx[None, :]
    mask = diff > 0 if strict else diff >= 0
    lg = log_g.astype(jnp.float32)
    D = jnp.where(mask[None], jnp.exp(jnp.where(mask, diff, 0.0)[None] * lg[:, None, None]), 0.0)
    scores = jnp.einsum('bhncd,bhnmd->bhncm', qc, kc) * D[None, :, None]
    inner = jnp.einsum('bhncm,bhnme->bhnce', scores, vc)
    zeta = jnp.exp((CHUNK - 1 - idx)[None, :] * lg[:, None])
    xi = jnp.exp((idx + 1)[None, :] * lg[:, None])
    kv = jnp.einsum('bhncd,bhnce->nbhde', kc * zeta[None, :, None, :, None], vc)
    chunk_decay = jnp.exp(CHUNK * lg)[None, :, None, None]

    def step(R, kv_n):
        return chunk_decay * R + kv_n, R

    R0 = jnp.zeros((B, H, d, d), jnp.float32)
    _, R_prev = lax.scan(step, R0, kv)
    cross = jnp.einsum('bhncd,nbhde->bhnce', qc, R_prev) * xi[None, :, None, :, None]
    return (inner + cross).reshape(B, H, S, d)


def retention_mixer(q, k, v, g, log_g_fwd, log_g_bwd, norm_w):
    B, S, _ = q.shape
    dt = q.dtype

    def heads(t):
        return t.astype(jnp.float32).reshape(B, S, RET_HEADS, RET_HEAD_DIM).transpose(0, 2, 1, 3)

    qh = rotary(heads(q))
    kh = rotary(heads(k)) * (RET_HEAD_DIM ** -0.5)
    vh = heads(v)
    fwd = retention_direction(qh, kh, vh, log_g_fwd, False)
    bwd = jnp.flip(retention_direction(jnp.flip(qh, 2), jnp.flip(kh, 2), jnp.flip(vh, 2), log_g_bwd, True), 2)
    o = fwd + bwd
    o = o * lax.rsqrt(jnp.mean(o * o, axis=-1, keepdims=True) + EPS)
    o = o * norm_w.astype(jnp.float32).reshape(RET_HEADS, 1, RET_HEAD_DIM)[None]
    o = o.transpose(0, 2, 1, 3).reshape(B, S, RET_WIDTH).astype(dt)
    return jax.nn.silu(g) * o


def block(x, norm1_w, w_in, conv_w, conv_b, conv_ln_w, conv_ln_b,
          ret_log_decay_fwd, ret_log_decay_bwd, ret_norm_w, w_out,
          norm2_w, w_ffn_in, w_ffn_out):
    h = rms_norm(x, norm1_w)
    u = h @ w_in.astype(h.dtype)
    c0 = 2 * CONV_WIDTH
    conv_u = u[..., :c0]
    q = u[..., c0:c0 + RET_WIDTH]
    k = u[..., c0 + RET_WIDTH:c0 + 2 * RET_WIDTH]
    v = u[..., c0 + 2 * RET_WIDTH:c0 + 3 * RET_WIDTH]
    g = u[..., c0 + 3 * RET_WIDTH:]
    y_conv = conv_mixer(conv_u, conv_w, conv_b, conv_ln_w, conv_ln_b)
    y_ret = retention_mixer(q, k, v, g, ret_log_decay_fwd, ret_log_decay_bwd, ret_norm_w)
    x = x + jnp.concatenate([y_conv, y_ret], axis=-1) @ w_out.astype(x.dtype)
    h2 = rms_norm(x, norm2_w)
    gu = h2 @ w_ffn_in.astype(h2.dtype)
    x = x + (jax.nn.silu(gu[..., :D_FF]) * gu[..., D_FF:]) @ w_ffn_out.astype(x.dtype)
    return x


def trunk(x, norm1_w, w_in, conv_w, conv_b, conv_ln_w, conv_ln_b,
          ret_log_decay_fwd, ret_log_decay_bwd, ret_norm_w, w_out,
          norm2_w, w_ffn_in, w_ffn_out, final_norm_w):
    for l in range(DEPTH):
        x = block(x, norm1_w[l], w_in[l], conv_w[l], conv_b[l], conv_ln_w[l], conv_ln_b[l],
                  ret_log_decay_fwd[l], ret_log_decay_bwd[l], ret_norm_w[l], w_out[l],
                  norm2_w[l], w_ffn_in[l], w_ffn_out[l])
    return rms_norm(x, final_norm_w)


def setup_inputs(seed: int = 0) -> dict:
    key = jax.random.key(seed)
    ks = jax.random.split(key, 20)
    f32 = jnp.float32
    base = jnp.log(1.0 - 2.0 ** (-5.0 - jnp.arange(RET_HEADS, dtype=f32)))
    return {
        "x_prompt": jax.random.normal(ks[0], (BATCH, SEQ, D_MODEL), f32),
        "x_sample": jax.random.normal(ks[1], (DEC_BATCH, DEC_SEQ, D_MODEL), f32),
        "norm1_w": 1.0 + 0.01 * jax.random.normal(ks[2], (DEPTH, D_MODEL), f32),
        "w_in": jax.random.normal(ks[3], (DEPTH, D_MODEL, IN_COLS), f32) * D_MODEL ** -0.5,
        "conv_w": jax.random.normal(ks[4], (DEPTH, CONV_KERNEL, CONV_WIDTH), f32) * CONV_KERNEL ** -0.5,
        "conv_b": 0.01 * jax.random.normal(ks[5], (DEPTH, CONV_WIDTH), f32),
        "conv_ln_w": 1.0 + 0.01 * jax.random.normal(ks[6], (DEPTH, CONV_WIDTH), f32),
        "conv_ln_b": 0.01 * jax.random.normal(ks[7], (DEPTH, CONV_WIDTH), f32),
        "ret_log_decay_fwd": base[None] * (1.0 + 0.1 * jax.random.uniform(ks[8], (DEPTH, RET_HEADS), f32)),
        "ret_log_decay_bwd": base[None] * (1.0 + 0.1 * jax.random.uniform(ks[9], (DEPTH, RET_HEADS), f32)),
        "ret_norm_w": 1.0 + 0.01 * jax.random.normal(ks[10], (DEPTH, RET_WIDTH), f32),
        "w_out": jax.random.normal(ks[11], (DEPTH, MIX_WIDTH, D_MODEL), f32) * MIX_WIDTH ** -0.5,
        "norm2_w": 1.0 + 0.01 * jax.random.normal(ks[12], (DEPTH, D_MODEL), f32),
        "w_ffn_in": jax.random.normal(ks[13], (DEPTH, D_MODEL, 2 * D_FF), f32) * D_MODEL ** -0.5,
        "w_ffn_out": jax.random.normal(ks[14], (DEPTH, D_FF, D_MODEL), f32) * D_FF ** -0.5,
        "final_norm_w": 1.0 + 0.01 * jax.random.normal(ks[15], (D_MODEL,), f32),
    }


def reference(x_prompt, x_sample, norm1_w, w_in, conv_w, conv_b, conv_ln_w, conv_ln_b,
              ret_log_decay_fwd, ret_log_decay_bwd, ret_norm_w, w_out,
              norm2_w, w_ffn_in, w_ffn_out, final_norm_w):
    y_prompt = trunk(x_prompt, norm1_w, w_in, conv_w, conv_b, conv_ln_w, conv_ln_b,
                     ret_log_decay_fwd, ret_log_decay_bwd, ret_norm_w, w_out,
                     norm2_w, w_ffn_in, w_ffn_out, final_norm_w)
    y_sample = trunk(x_sample, norm1_w, w_in, conv_w, conv_b, conv_ln_w, conv_ln_b,
                     ret_log_decay_fwd, ret_log_decay_bwd, ret_norm_w, w_out,
                     norm2_w, w_ffn_in, w_ffn_out, final_norm_w)
    return (y_prompt, y_sample)
```

```python
import functools

import jax
import jax.numpy as jnp
from jax import lax
from jax.experimental import pallas as pl
from jax.experimental.pallas import tpu as pltpu

F32 = jnp.float32
BF16 = jnp.bfloat16

EPS = 1e-6
ROPE_BASE = 10000.0
RET_HEADS = 8
CONV_KERNEL = 31
CONV_PAD = CONV_KERNEL // 2
HALO = 16
RET_CHUNK = 256
VMEM_LIMIT = 56 * 1024 * 1024


def _sigmoid(x):
    return 1.0 / (1.0 + jnp.exp(-x))


def _rms_rows(x, w):
    ms = jnp.mean(x * x, axis=-1, keepdims=True)
    return x * lax.rsqrt(ms + EPS) * w


def _in_proj_kernel(x_ref, nw_ref, w_ref, u_ref, h_ref, *, row_chunk):
    tm = x_ref.shape[0]

    @pl.when(pl.program_id(1) == 0)
    def _():
        def body(c, carry):
            r = pl.multiple_of(c * row_chunk, row_chunk)
            x = x_ref[pl.ds(r, row_chunk), :]
            h_ref[pl.ds(r, row_chunk), :] = _rms_rows(x, nw_ref[...]).astype(BF16)
            return carry
        lax.fori_loop(0, tm // row_chunk, body, 0)

    u_ref[...] = jnp.dot(h_ref[...], w_ref[...], preferred_element_type=F32)


def _in_proj(x2, norm_w, w_bf, *, tm=1024, tn=1024):
    T, D = x2.shape
    N = w_bf.shape[1]
    tm = min(tm, T)
    return pl.pallas_call(
        functools.partial(_in_proj_kernel, row_chunk=128),
        grid=(T // tm, N // tn),
        in_specs=[
            pl.BlockSpec((tm, D), lambda i, j: (i, 0)),
            pl.BlockSpec((1, D), lambda i, j: (0, 0)),
            pl.BlockSpec((D, tn), lambda i, j: (0, j)),
        ],
        out_specs=pl.BlockSpec((tm, tn), lambda i, j: (i, j)),
        out_shape=jax.ShapeDtypeStruct((T, N), F32),
        scratch_shapes=[pltpu.VMEM((tm, D), BF16)],
        compiler_params=pltpu.CompilerParams(
            dimension_semantics=("parallel", "arbitrary"),
            vmem_limit_bytes=VMEM_LIMIT),
        name="in_proj",
    )(x2, norm_w.reshape(1, D), w_bf)


def _retention_kernel(lg_ref, q_ref, k_ref, v_ref, g_ref, cos_ref, sin_ref, nw_ref,
                      y_ref, qs_ref, qx_ref, kr_ref, vb_ref, kv_ref, r_ref,
                      *, chunk):
    S, d = q_ref.shape
    C = chunk
    n_chunks = S // C
    h = pl.program_id(1)
    lg_f = lg_ref[0, h]
    lg_b = lg_ref[1, h]

    pos = lax.broadcasted_iota(jnp.int32, (C, d), 0).astype(F32)
    zeta_f = jnp.exp((C - 1 - pos) * lg_f)
    xi_f = jnp.exp((pos + 1.0) * lg_f)
    zeta_b = jnp.exp(pos * lg_b)
    xi_b = jnp.exp((C - pos) * lg_b)
    row = lax.broadcasted_iota(jnp.int32, (C, C), 0)
    col = lax.broadcasted_iota(jnp.int32, (C, C), 1)
    diff = (row - col).astype(F32)
    decay = jnp.exp(jnp.where(diff >= 0, diff * lg_f, -diff * lg_b))
    scale = d ** -0.5

    def rot(x, cos, sin):
        return x * cos + pltpu.roll(x, d // 2, axis=1) * sin

    def prep(n, carry):
        r = pl.multiple_of(n * C, C)
        cos = cos_ref[pl.ds(r, C), :]
        sin = sin_ref[pl.ds(r, C), :]
        q = rot(q_ref[pl.ds(r, C), :], cos, sin)
        k = rot(k_ref[pl.ds(r, C), :], cos, sin) * scale
        qs_ref[pl.ds(r, C), :] = q.astype(BF16)
        qx_ref[pl.ds(r, C), :] = jnp.concatenate([q * xi_f, q * xi_b], axis=1).astype(BF16)
        kr_ref[pl.ds(r, C), :] = k.astype(BF16)
        kz = jnp.concatenate([k * zeta_f, k * zeta_b], axis=1).astype(BF16)
        vb = v_ref[pl.ds(r, C), :].astype(BF16)
        vb_ref[pl.ds(r, C), :] = vb
        kv_ref[n] = lax.dot_general(kz, vb, (((0,), (0,)), ((), ())),
                                    preferred_element_type=F32)
        return carry
    lax.fori_loop(0, n_chunks, prep, 0)

    dec_f = jnp.exp(C * lg_f)
    dec_b = jnp.exp(C * lg_b)

    def scan_f(n, state):
        r_ref[n, 0:d, :] = state.astype(BF16)
        return dec_f * state + kv_ref[n, 0:d, :]
    lax.fori_loop(0, n_chunks, scan_f, jnp.zeros((d, d), F32))

    def scan_b(t, state):
        n = n_chunks - 1 - t
        r_ref[n, d:2 * d, :] = state.astype(BF16)
        return dec_b * state + kv_ref[n, d:2 * d, :]
    lax.fori_loop(0, n_chunks, scan_b, jnp.zeros((d, d), F32))

    nw = nw_ref[...]

    def out(n, carry):
        r = pl.multiple_of(n * C, C)
        q = qs_ref[pl.ds(r, C), :]
        s = lax.dot_general(q, kr_ref[pl.ds(r, C), :], (((1,), (1,)), ((), ())),
                            preferred_element_type=F32)
        p = (s * decay).astype(BF16)
        o = jnp.dot(p, vb_ref[pl.ds(r, C), :], preferred_element_type=F32)
        o = o + jnp.dot(qx_ref[pl.ds(r, C), :], r_ref[n], preferred_element_type=F32)
        o = _rms_rows(o, nw)
        g = g_ref[pl.ds(r, C), :]
        y_ref[pl.ds(r, C), :] = (g * _sigmoid(g) * o).astype(y_ref.dtype)
        return carry
    lax.fori_loop(0, n_chunks, out, 0)


def _retention(u3, lg, cos_t, sin_t, norm_w, *, width, col0):
    B, S, _ = u3.shape
    H = RET_HEADS
    d = width // H
    C = RET_CHUNK
    blk0 = col0 // d
    per = width // d

    def col_spec(group):
        return pl.BlockSpec((None, S, d), lambda b, h, g=group: (b, 0, blk0 + g * per + h))

    return pl.pallas_call(
        functools.partial(_retention_kernel, chunk=C),
        grid=(B, H),
        in_specs=[
            pl.BlockSpec(memory_space=pltpu.SMEM),
            col_spec(0), col_spec(1), col_spec(2), col_spec(3),
            pl.BlockSpec((S, d), lambda b, h: (0, 0)),
            pl.BlockSpec((S, d), lambda b, h: (0, 0)),
            pl.BlockSpec((1, d), lambda b, h: (0, h)),
        ],
        out_specs=pl.BlockSpec((None, S, d), lambda b, h: (b, 0, h)),
        out_shape=jax.ShapeDtypeStruct((B, S, width), BF16),
        scratch_shapes=[
            pltpu.VMEM((S, d), BF16),
            pltpu.VMEM((S, 2 * d), BF16),
            pltpu.VMEM((S, d), BF16),
            pltpu.VMEM((S, d), BF16),
            pltpu.VMEM((S // C, 2 * d, d), F32),
            pltpu.VMEM((S // C, 2 * d, d), BF16),
        ],
        compiler_params=pltpu.CompilerParams(
            dimension_semantics=("parallel", "arbitrary"),
            vmem_limit_bytes=VMEM_LIMIT),
        name="retention",
    )(lg, u3, u3, u3, u3, cos_t, sin_t, norm_w.reshape(1, width))


def _conv_kernel(a_ref, g_ref, hp_ref, hn_ref, cw_ref, cb_ref, lw_ref, lb_ref,
                 y_ref, hbuf_ref, *, row_block, lane_block):
    tm, W = a_ref.shape
    i = pl.program_id(1)
    last = pl.num_programs(1) - 1

    def glu(a, g):
        return a * _sigmoid(g)

    hbuf_ref[HALO:HALO + tm, :] = glu(a_ref[...], g_ref[...])
    prev = glu(hp_ref[:, 0:W], hp_ref[:, W:2 * W])
    hbuf_ref[0:HALO, :] = jnp.where(i > 0, prev, 0.0)
    nxt = glu(hn_ref[:, 0:W], hn_ref[:, W:2 * W])
    hbuf_ref[HALO + tm:2 * HALO + tm, :] = jnp.where(i < last, nxt, 0.0)

    off0 = HALO - CONV_PAD
    for rb in range(tm // row_block):
        r0 = rb * row_block
        parts = []
        for lb in range(W // lane_block):
            ls = slice(lb * lane_block, (lb + 1) * lane_block)
            acc = jnp.zeros((row_block, lane_block), F32)
            for k in range(CONV_KERNEL):
                acc = acc + hbuf_ref[r0 + off0 + k:r0 + off0 + k + row_block, ls] * cw_ref[k:k + 1, ls]
            parts.append(acc + cb_ref[:, ls])
        c = jnp.concatenate(parts, axis=1)
        mu = jnp.mean(c, axis=-1, keepdims=True)
        cc = c - mu
        var = jnp.mean(cc * cc, axis=-1, keepdims=True)
        z = cc * lax.rsqrt(var + EPS) * lw_ref[...] + lb_ref[...]
        y_ref[r0:r0 + row_block, :] = (z * _sigmoid(z)).astype(y_ref.dtype)


def _conv_mixer(u3, conv_w, conv_b, ln_w, ln_b, *, width, tm=256):
    B, S, _ = u3.shape
    W = width
    hb = tm // HALO
    n_halo = S // HALO
    return pl.pallas_call(
        functools.partial(_conv_kernel, row_block=32, lane_block=256),
        grid=(B, S // tm),
        in_specs=[
            pl.BlockSpec((None, tm, W), lambda b, i: (b, i, 0)),
            pl.BlockSpec((None, tm, W), lambda b, i: (b, i, 1)),
            pl.BlockSpec((None, HALO, 2 * W), lambda b, i: (b, jnp.maximum(i * hb - 1, 0), 0)),
            pl.BlockSpec((None, HALO, 2 * W), lambda b, i: (b, jnp.minimum((i + 1) * hb, n_halo - 1), 0)),
            pl.BlockSpec((CONV_KERNEL, W), lambda b, i: (0, 0)),
            pl.BlockSpec((1, W), lambda b, i: (0, 0)),
            pl.BlockSpec((1, W), lambda b, i: (0, 0)),
            pl.BlockSpec((1, W), lambda b, i: (0, 0)),
        ],
        out_specs=pl.BlockSpec((None, tm, W), lambda b, i: (b, i, 0)),
        out_shape=jax.ShapeDtypeStruct((B, S, W), BF16),
        scratch_shapes=[pltpu.VMEM((tm + 2 * HALO, W), F32)],
        compiler_params=pltpu.CompilerParams(
            dimension_semantics=("parallel", "arbitrary"),
            vmem_limit_bytes=VMEM_LIMIT),
        name="conv_mixer",
    )(u3, u3, u3, u3, conv_w, conv_b.reshape(1, W), ln_w.reshape(1, W), ln_b.reshape(1, W))


def _out_proj_kernel(x_ref, yc_ref, yr_ref, w_ref, o_ref):
    wc = yc_ref.shape[1]
    acc = jnp.dot(yc_ref[...], w_ref[0:wc, :], preferred_element_type=F32)
    acc = acc + jnp.dot(yr_ref[...], w_ref[wc:, :], preferred_element_type=F32)
    o_ref[...] = x_ref[...] + acc


def _out_proj(x2, yc2, yr2, w_bf, *, tm=512):
    T, D = x2.shape
    wc, wr = yc2.shape[1], yr2.shape[1]
    tm = min(tm, T)
    return pl.pallas_call(
        _out_proj_kernel,
        grid=(T // tm,),
        in_specs=[
            pl.BlockSpec((tm, D), lambda i: (i, 0)),
            pl.BlockSpec((tm, wc), lambda i: (i, 0)),
            pl.BlockSpec((tm, wr), lambda i: (i, 0)),
            pl.BlockSpec((wc + wr, D), lambda i: (0, 0)),
        ],
        out_specs=pl.BlockSpec((tm, D), lambda i: (i, 0)),
        out_shape=jax.ShapeDtypeStruct((T, D), F32),
        compiler_params=pltpu.CompilerParams(
            dimension_semantics=("parallel",),
            vmem_limit_bytes=VMEM_LIMIT),
        name="out_proj",
    )(x2, yc2, yr2, w_bf)


def _ffn_kernel(x_ref, n2_ref, wg_ref, wu_ref, wo_ref, fn_ref, o_ref, h_ref,
                *, row_chunk, final_norm):
    tm = x_ref.shape[0]
    j = pl.program_id(1)

    @pl.when(j == 0)
    def _():
        def body(c, carry):
            r = pl.multiple_of(c * row_chunk, row_chunk)
            x = x_ref[pl.ds(r, row_chunk), :]
            h_ref[pl.ds(r, row_chunk), :] = _rms_rows(x, n2_ref[...]).astype(BF16)
            o_ref[pl.ds(r, row_chunk), :] = x
            return carry
        lax.fori_loop(0, tm // row_chunk, body, 0)

    h = h_ref[...]
    g = jnp.dot(h, wg_ref[...], preferred_element_type=F32)
    u = jnp.dot(h, wu_ref[...], preferred_element_type=F32)
    a = (g * _sigmoid(g) * u).astype(BF16)
    o_ref[...] += jnp.dot(a, wo_ref[...], preferred_element_type=F32)

    if final_norm:
        @pl.when(j == pl.num_programs(1) - 1)
        def _():
            def body(c, carry):
                r = pl.multiple_of(c * row_chunk, row_chunk)
                o_ref[pl.ds(r, row_chunk), :] = _rms_rows(o_ref[pl.ds(r, row_chunk), :], fn_ref[...])
                return carry
            lax.fori_loop(0, tm // row_chunk, body, 0)


def _ffn(x2, norm_w, w_in_bf, w_out_bf, final_w, *, final_norm, tm=512, tf=512):
    T, D = x2.shape
    F = w_out_bf.shape[0]
    tm = min(tm, T)
    nf = F // tf
    return pl.pallas_call(
        functools.partial(_ffn_kernel, row_chunk=128, final_norm=final_norm),
        grid=(T // tm, nf),
        in_specs=[
            pl.BlockSpec((tm, D), lambda i, j: (i, 0)),
            pl.BlockSpec((1, D), lambda i, j: (0, 0)),
            pl.BlockSpec((D, tf), lambda i, j: (0, j)),
            pl.BlockSpec((D, tf), lambda i, j: (0, j + nf)),
            pl.BlockSpec((tf, D), lambda i, j: (j, 0)),
            pl.BlockSpec((1, D), lambda i, j: (0, 0)),
        ],
        out_specs=pl.BlockSpec((tm, D), lambda i, j: (i, 0)),
        out_shape=jax.ShapeDtypeStruct((T, D), F32),
        scratch_shapes=[pltpu.VMEM((tm, D), BF16)],
        compiler_params=pltpu.CompilerParams(
            dimension_semantics=("parallel", "arbitrary"),
            vmem_limit_bytes=VMEM_LIMIT),
        name="ffn",
    )(x2, norm_w.reshape(1, D), w_in_bf, w_in_bf, w_out_bf, final_w.reshape(1, D))


def _rope_tables(S, d):
    inv_freq = ROPE_BASE ** (-jnp.arange(0, d, 2, dtype=F32) / d)
    ang = jnp.arange(S, dtype=F32)[:, None] * inv_freq[None, :]
    cos, sin = jnp.cos(ang), jnp.sin(ang)
    return jnp.concatenate([cos, cos], axis=1), jnp.concatenate([-sin, sin], axis=1)


def _trunk(x, p):
    B, S, D = x.shape
    depth = p["w_in_bf"].shape[0]
    conv_width = p["conv_w"].shape[-1]
    ret_width = p["ret_norm_w"].shape[-1]
    cos_t, sin_t = _rope_tables(S, ret_width // RET_HEADS)
    x2 = x.reshape(B * S, D)
    for l in range(depth):
        last = l == depth - 1
        u = _in_proj(x2, p["norm1_w"][l], p["w_in_bf"][l])
        u3 = u.reshape(B, S, u.shape[-1])
        lg = jnp.stack([p["ret_log_decay_fwd"][l], p["ret_log_decay_bwd"][l]]).astype(F32)
        yr = _retention(u3, lg, cos_t, sin_t, p["ret_norm_w"][l],
                        width=ret_width, col0=2 * conv_width)
        yc = _conv_mixer(u3, p["conv_w"][l], p["conv_b"][l], p["conv_ln_w"][l], p["conv_ln_b"][l],
                         width=conv_width)
        x2 = _out_proj(x2, yc.reshape(B * S, conv_width), yr.reshape(B * S, ret_width),
                       p["w_out_bf"][l])
        x2 = _ffn(x2, p["norm2_w"][l], p["w_ffn_in_bf"][l], p["w_ffn_out_bf"][l],
                  p["final_norm_w"], final_norm=last)
    return x2.reshape(B, S, D)


def kernel(x_prompt, x_sample, norm1_w, w_in, conv_w, conv_b, conv_ln_w, conv_ln_b,
           ret_log_decay_fwd, ret_log_decay_bwd, ret_norm_w, w_out,
           norm2_w, w_ffn_in, w_ffn_out, final_norm_w):
    p = dict(norm1_w=norm1_w, conv_w=conv_w, conv_b=conv_b, conv_ln_w=conv_ln_w,
             conv_ln_b=conv_ln_b, ret_log_decay_fwd=ret_log_decay_fwd,
             ret_log_decay_bwd=ret_log_decay_bwd, ret_norm_w=ret_norm_w,
             norm2_w=norm2_w, final_norm_w=final_norm_w,
             w_in_bf=w_in.astype(BF16), w_out_bf=w_out.astype(BF16),
             w_ffn_in_bf=w_ffn_in.astype(BF16), w_ffn_out_bf=w_ffn_out.astype(BF16))
    return _trunk(x_prompt, p), _trunk(x_sample, p)
```

```python
import functools

import jax
import jax.numpy as jnp
from jax import lax
from jax.experimental import pallas as pl
from jax.experimental.pallas import tpu as pltpu

F32 = jnp.float32
BF16 = jnp.bfloat16

EPS = 1e-6
ROPE_BASE = 10000.0
RET_HEADS = 8
CONV_KERNEL = 31
CONV_PAD = CONV_KERNEL // 2
SUBLANES = 8
LANES = 128
HALO = 16
RET_CHUNK = 256
VMEM_LIMIT = 56 * 1024 * 1024


def _sigmoid(x):
    return 1.0 / (1.0 + jnp.exp(-x))


def _rms_rows(x, w):
    ms = jnp.mean(x * x, axis=-1, keepdims=True)
    return x * lax.rsqrt(ms + EPS) * w


def _in_proj_kernel(x_ref, nw_ref, w_ref, u_ref, h_ref, *, row_chunk):
    tm = x_ref.shape[0]

    @pl.when(pl.program_id(1) == 0)
    def _():
        def body(c, carry):
            r = pl.multiple_of(c * row_chunk, row_chunk)
            x = x_ref[pl.ds(r, row_chunk), :]
            h_ref[pl.ds(r, row_chunk), :] = _rms_rows(x, nw_ref[...]).astype(BF16)
            return carry
        lax.fori_loop(0, tm // row_chunk, body, 0)

    u_ref[...] = jnp.dot(h_ref[...], w_ref[...], preferred_element_type=F32)


def _in_proj(x2, norm_w, w_bf, *, tm=1024, tn=1024):
    T, D = x2.shape
    N = w_bf.shape[1]
    tm = min(tm, T)
    return pl.pallas_call(
        functools.partial(_in_proj_kernel, row_chunk=128),
        grid=(T // tm, N // tn),
        in_specs=[
            pl.BlockSpec((tm, D), lambda i, j: (i, 0)),
            pl.BlockSpec((1, D), lambda i, j: (0, 0)),
            pl.BlockSpec((D, tn), lambda i, j: (0, j)),
        ],
        out_specs=pl.BlockSpec((tm, tn), lambda i, j: (i, j)),
        out_shape=jax.ShapeDtypeStruct((T, N), F32),
        scratch_shapes=[pltpu.VMEM((tm, D), BF16)],
        compiler_params=pltpu.CompilerParams(
            dimension_semantics=("parallel", "arbitrary"),
            vmem_limit_bytes=VMEM_LIMIT),
        name="in_proj",
    )(x2, norm_w.reshape(1, D), w_bf)


def _retention_kernel(lg_ref, q_ref, k_ref, v_ref, g_ref, cos_ref, sin_ref, nw_ref,
                      y_ref, qs_ref, qx_ref, kr_ref, vb_ref, kv_ref, r_ref,
                      *, chunk):
    S, d = q_ref.shape
    C = chunk
    n_chunks = S // C
    h = pl.program_id(1)
    lg_f = lg_ref[0, h]
    lg_b = lg_ref[1, h]

    pos = lax.broadcasted_iota(jnp.int32, (C, d), 0).astype(F32)
    zeta_f = jnp.exp((C - 1 - pos) * lg_f)
    xi_f = jnp.exp((pos + 1.0) * lg_f)
    zeta_b = jnp.exp(pos * lg_b)
    xi_b = jnp.exp((C - pos) * lg_b)
    row = lax.broadcasted_iota(jnp.int32, (C, C), 0)
    col = lax.broadcasted_iota(jnp.int32, (C, C), 1)
    diff = (row - col).astype(F32)
    decay = jnp.exp(jnp.where(diff >= 0, diff * lg_f, -diff * lg_b))
    scale = d ** -0.5

    def rot(x, cos, sin):
        return x * cos + pltpu.roll(x, d // 2, axis=1) * sin

    def prep(n, carry):
        r = pl.multiple_of(n * C, C)
        cos = cos_ref[pl.ds(r, C), :]
        sin = sin_ref[pl.ds(r, C), :]
        q = rot(q_ref[pl.ds(r, C), :], cos, sin)
        k = rot(k_ref[pl.ds(r, C), :], cos, sin) * scale
        qs_ref[pl.ds(r, C), :] = q.astype(BF16)
        qx_ref[pl.ds(r, C), :] = jnp.concatenate([q * xi_f, q * xi_b], axis=1).astype(BF16)
        kr_ref[pl.ds(r, C), :] = k.astype(BF16)
        kz = jnp.concatenate([k * zeta_f, k * zeta_b], axis=1).astype(BF16)
        vb = v_ref[pl.ds(r, C), :].astype(BF16)
        vb_ref[pl.ds(r, C), :] = vb
        kv_ref[n] = lax.dot_general(kz, vb, (((0,), (0,)), ((), ())),
                                    preferred_element_type=F32)
        return carry
    lax.fori_loop(0, n_chunks, prep, 0, unroll=2)

    dec_f = jnp.exp(C * lg_f)
    dec_b = jnp.exp(C * lg_b)

    def scan_f(n, state):
        r_ref[n, 0:d, :] = state.astype(BF16)
        return dec_f * state + kv_ref[n, 0:d, :]
    lax.fori_loop(0, n_chunks, scan_f, jnp.zeros((d, d), F32))

    def scan_b(t, state):
        n = n_chunks - 1 - t
        r_ref[n, d:2 * d, :] = state.astype(BF16)
        return dec_b * state + kv_ref[n, d:2 * d, :]
    lax.fori_loop(0, n_chunks, scan_b, jnp.zeros((d, d), F32))

    nw = nw_ref[...]

    def out(n, carry):
        r = pl.multiple_of(n * C, C)
        q = qs_ref[pl.ds(r, C), :]
        s = lax.dot_general(q, kr_ref[pl.ds(r, C), :], (((1,), (1,)), ((), ())),
                            preferred_element_type=F32)
        p = (s * decay).astype(BF16)
        o = jnp.dot(p, vb_ref[pl.ds(r, C), :], preferred_element_type=F32)
        o = o + jnp.dot(qx_ref[pl.ds(r, C), :], r_ref[n], preferred_element_type=F32)
        o = _rms_rows(o, nw)
        g = g_ref[pl.ds(r, C), :]
        y_ref[pl.ds(r, C), :] = (g * _sigmoid(g) * o).astype(y_ref.dtype)
        return carry
    lax.fori_loop(0, n_chunks, out, 0, unroll=2)


def _retention(u3, lg, cos_t, sin_t, norm_w, *, width, col0):
    B, S, _ = u3.shape
    H = RET_HEADS
    d = width // H
    C = RET_CHUNK
    blk0 = col0 // d
    per = width // d

    def col_spec(group):
        return pl.BlockSpec((None, S, d), lambda b, h, g=group: (b, 0, blk0 + g * per + h))

    return pl.pallas_call(
        functools.partial(_retention_kernel, chunk=C),
        grid=(B, H),
        in_specs=[
            pl.BlockSpec(memory_space=pltpu.SMEM),
            col_spec(0), col_spec(1), col_spec(2), col_spec(3),
            pl.BlockSpec((S, d), lambda b, h: (0, 0)),
            pl.BlockSpec((S, d), lambda b, h: (0, 0)),
            pl.BlockSpec((1, d), lambda b, h: (0, h)),
        ],
        out_specs=pl.BlockSpec((None, S, d), lambda b, h: (b, 0, h)),
        out_shape=jax.ShapeDtypeStruct((B, S, width), BF16),
        scratch_shapes=[
            pltpu.VMEM((S, d), BF16),
            pltpu.VMEM((S, 2 * d), BF16),
            pltpu.VMEM((S, d), BF16),
            pltpu.VMEM((S, d), BF16),
            pltpu.VMEM((S // C, 2 * d, d), F32),
            pltpu.VMEM((S // C, 2 * d, d), BF16),
        ],
        compiler_params=pltpu.CompilerParams(
            dimension_semantics=("parallel", "arbitrary"),
            vmem_limit_bytes=VMEM_LIMIT),
        name="retention",
    )(lg, u3, u3, u3, u3, cos_t, sin_t, norm_w.reshape(1, width))


def _conv_kernel(a_ref, g_ref, hp_ref, hn_ref, cw_ref, cb_ref, lw_ref, lb_ref,
                 y_ref, xs_ref, cv_ref, *, row_block, norm_block):
    tm, W = a_ref.shape
    n_lt = W // LANES
    i = pl.program_id(1)
    last = pl.num_programs(1) - 1
    off0 = HALO - CONV_PAD

    def glu(a, g):
        return a * _sigmoid(g)

    for t in range(n_lt):
        la = slice(t * LANES, (t + 1) * LANES)
        lg = slice(W + t * LANES, W + (t + 1) * LANES)
        xs_ref[t, HALO:HALO + tm, :] = glu(a_ref[:, la], g_ref[:, la])
        xs_ref[t, 0:HALO, :] = jnp.where(i > 0, glu(hp_ref[:, la], hp_ref[:, lg]), 0.0)
        xs_ref[t, HALO + tm:2 * HALO + tm, :] = jnp.where(
            i < last, glu(hn_ref[:, la], hn_ref[:, lg]), 0.0)

    reps = row_block // SUBLANES

    def conv_tile(idx, carry):
        rb = idx // n_lt
        t = idx % n_lt
        r0 = rb * row_block
        acc = jnp.concatenate([cb_ref[t]] * reps, axis=0)
        for k in range(CONV_KERNEL):
            wk = jnp.concatenate([cw_ref[t, k]] * reps, axis=0)
            acc = acc + xs_ref[t, pl.ds(r0 + (off0 + k), row_block), :] * wk
        cv_ref[t, pl.ds(pl.multiple_of(r0, row_block), row_block), :] = acc
        return carry
    lax.fori_loop(0, (tm // row_block) * n_lt, conv_tile, 0)

    def norm_rows(rb, carry):
        r0 = pl.multiple_of(rb * norm_block, norm_block)
        c = jnp.concatenate([cv_ref[t, pl.ds(r0, norm_block), :] for t in range(n_lt)], axis=1)
        mu = jnp.mean(c, axis=-1, keepdims=True)
        cc = c - mu
        var = jnp.mean(cc * cc, axis=-1, keepdims=True)
        z = cc * lax.rsqrt(var + EPS) * lw_ref[...] + lb_ref[...]
        y_ref[pl.ds(r0, norm_block), :] = (z * _sigmoid(z)).astype(y_ref.dtype)
        return carry
    lax.fori_loop(0, tm // norm_block, norm_rows, 0, unroll=4)


def _conv_mixer(u3, conv_w, conv_b, ln_w, ln_b, *, width, tm=256):
    B, S, _ = u3.shape
    W = width
    n_lt = W // LANES
    hb = tm // HALO
    n_halo = S // HALO
    cw8 = jnp.broadcast_to(conv_w.reshape(CONV_KERNEL, n_lt, 1, LANES).transpose(1, 0, 2, 3),
                           (n_lt, CONV_KERNEL, SUBLANES, LANES))
    cb8 = jnp.broadcast_to(conv_b.reshape(n_lt, 1, LANES), (n_lt, SUBLANES, LANES))
    return pl.pallas_call(
        functools.partial(_conv_kernel, row_block=128, norm_block=32),
        grid=(B, S // tm),
        in_specs=[
            pl.BlockSpec((None, tm, W), lambda b, i: (b, i, 0)),
            pl.BlockSpec((None, tm, W), lambda b, i: (b, i, 1)),
            pl.BlockSpec((None, HALO, 2 * W), lambda b, i: (b, jnp.maximum(i * hb - 1, 0), 0)),
            pl.BlockSpec((None, HALO, 2 * W), lambda b, i: (b, jnp.minimum((i + 1) * hb, n_halo - 1), 0)),
            pl.BlockSpec((n_lt, CONV_KERNEL, SUBLANES, LANES), lambda b, i: (0, 0, 0, 0)),
            pl.BlockSpec((n_lt, SUBLANES, LANES), lambda b, i: (0, 0, 0)),
            pl.BlockSpec((1, W), lambda b, i: (0, 0)),
            pl.BlockSpec((1, W), lambda b, i: (0, 0)),
        ],
        out_specs=pl.BlockSpec((None, tm, W), lambda b, i: (b, i, 0)),
        out_shape=jax.ShapeDtypeStruct((B, S, W), BF16),
        scratch_shapes=[pltpu.VMEM((n_lt, tm + 2 * HALO, LANES), F32),
                        pltpu.VMEM((n_lt, tm, LANES), F32)],
        compiler_params=pltpu.CompilerParams(
            dimension_semantics=("parallel", "arbitrary"),
            vmem_limit_bytes=VMEM_LIMIT),
        name="conv_mixer",
    )(u3, u3, u3, u3, cw8, cb8, ln_w.reshape(1, W), ln_b.reshape(1, W))


def _out_proj_kernel(x_ref, yc_ref, yr_ref, w_ref, o_ref):
    wc = yc_ref.shape[1]
    acc = jnp.dot(yc_ref[...], w_ref[0:wc, :], preferred_element_type=F32)
    acc = acc + jnp.dot(yr_ref[...], w_ref[wc:, :], preferred_element_type=F32)
    o_ref[...] = x_ref[...] + acc


def _out_proj(x2, yc2, yr2, w_bf, *, tm=512):
    T, D = x2.shape
    wc, wr = yc2.shape[1], yr2.shape[1]
    tm = min(tm, T)
    return pl.pallas_call(
        _out_proj_kernel,
        grid=(T // tm,),
        in_specs=[
            pl.BlockSpec((tm, D), lambda i: (i, 0)),
            pl.BlockSpec((tm, wc), lambda i: (i, 0)),
            pl.BlockSpec((tm, wr), lambda i: (i, 0)),
            pl.BlockSpec((wc + wr, D), lambda i: (0, 0)),
        ],
        out_specs=pl.BlockSpec((tm, D), lambda i: (i, 0)),
        out_shape=jax.ShapeDtypeStruct((T, D), F32),
        compiler_params=pltpu.CompilerParams(
            dimension_semantics=("parallel",),
            vmem_limit_bytes=VMEM_LIMIT),
        name="out_proj",
    )(x2, yc2, yr2, w_bf)


def _ffn_kernel(x_ref, n2_ref, wg_ref, wu_ref, wo_ref, fn_ref, o_ref, h_ref,
                *, row_chunk, final_norm):
    tm = x_ref.shape[0]
    j = pl.program_id(1)

    @pl.when(j == 0)
    def _():
        def body(c, carry):
            r = pl.multiple_of(c * row_chunk, row_chunk)
            x = x_ref[pl.ds(r, row_chunk), :]
            h_ref[pl.ds(r, row_chunk), :] = _rms_rows(x, n2_ref[...]).astype(BF16)
            o_ref[pl.ds(r, row_chunk), :] = x
            return carry
        lax.fori_loop(0, tm // row_chunk, body, 0)

    h = h_ref[...]
    g = jnp.dot(h, wg_ref[...], preferred_element_type=F32)
    u = jnp.dot(h, wu_ref[...], preferred_element_type=F32)
    a = (g * _sigmoid(g) * u).astype(BF16)
    o_ref[...] += jnp.dot(a, wo_ref[...], preferred_element_type=F32)

    if final_norm:
        @pl.when(j == pl.num_programs(1) - 1)
        def _():
            def body(c, carry):
                r = pl.multiple_of(c * row_chunk, row_chunk)
                o_ref[pl.ds(r, row_chunk), :] = _rms_rows(o_ref[pl.ds(r, row_chunk), :], fn_ref[...])
                return carry
            lax.fori_loop(0, tm // row_chunk, body, 0)


def _ffn(x2, norm_w, w_in_bf, w_out_bf, final_w, *, final_norm, tm=512, tf=512):
    T, D = x2.shape
    F = w_out_bf.shape[0]
    tm = min(tm, T)
    nf = F // tf
    return pl.pallas_call(
        functools.partial(_ffn_kernel, row_chunk=128, final_norm=final_norm),
        grid=(T // tm, nf),
        in_specs=[
            pl.BlockSpec((tm, D), lambda i, j: (i, 0)),
            pl.BlockSpec((1, D), lambda i, j: (0, 0)),
            pl.BlockSpec((D, tf), lambda i, j: (0, j)),
            pl.BlockSpec((D, tf), lambda i, j: (0, j + nf)),
            pl.BlockSpec((tf, D), lambda i, j: (j, 0)),
            pl.BlockSpec((1, D), lambda i, j: (0, 0)),
        ],
        out_specs=pl.BlockSpec((tm, D), lambda i, j: (i, 0)),
        out_shape=jax.ShapeDtypeStruct((T, D), F32),
        scratch_shapes=[pltpu.VMEM((tm, D), BF16)],
        compiler_params=pltpu.CompilerParams(
            dimension_semantics=("parallel", "arbitrary"),
            vmem_limit_bytes=VMEM_LIMIT),
        name="ffn",
    )(x2, norm_w.reshape(1, D), w_in_bf, w_in_bf, w_out_bf, final_w.reshape(1, D))


def _rope_tables(S, d):
    inv_freq = ROPE_BASE ** (-jnp.arange(0, d, 2, dtype=F32) / d)
    ang = jnp.arange(S, dtype=F32)[:, None] * inv_freq[None, :]
    cos, sin = jnp.cos(ang), jnp.sin(ang)
    return jnp.concatenate([cos, cos], axis=1), jnp.concatenate([-sin, sin], axis=1)


def _trunk(x, p):
    B, S, D = x.shape
    depth = p["w_in_bf"].shape[0]
    conv_width = p["conv_w"].shape[-1]
    ret_width = p["ret_norm_w"].shape[-1]
    cos_t, sin_t = _rope_tables(S, ret_width // RET_HEADS)
    x2 = x.reshape(B * S, D)
    for l in range(depth):
        last = l == depth - 1
        u = _in_proj(x2, p["norm1_w"][l], p["w_in_bf"][l])
        u3 = u.reshape(B, S, u.shape[-1])
        lg = jnp.stack([p["ret_log_decay_fwd"][l], p["ret_log_decay_bwd"][l]]).astype(F32)
        yr = _retention(u3, lg, cos_t, sin_t, p["ret_norm_w"][l],
                        width=ret_width, col0=2 * conv_width)
        yc = _conv_mixer(u3, p["conv_w"][l], p["conv_b"][l], p["conv_ln_w"][l], p["conv_ln_b"][l],
                         width=conv_width)
        x2 = _out_proj(x2, yc.reshape(B * S, conv_width), yr.reshape(B * S, ret_width),
                       p["w_out_bf"][l])
        x2 = _ffn(x2, p["norm2_w"][l], p["w_ffn_in_bf"][l], p["w_ffn_out_bf"][l],
                  p["final_norm_w"], final_norm=last)
    return x2.reshape(B, S, D)


def kernel(x_prompt, x_sample, norm1_w, w_in, conv_w, conv_b, conv_ln_w, conv_ln_b,
           ret_log_decay_fwd, ret_log_decay_bwd, ret_norm_w, w_out,
           norm2_w, w_ffn_in, w_ffn_out, final_norm_w):
    p = dict(norm1_w=norm1_w, conv_w=conv_w, conv_b=conv_b, conv_ln_w=conv_ln_w,
             conv_ln_b=conv_ln_b, ret_log_decay_fwd=ret_log_decay_fwd,
             ret_log_decay_bwd=ret_log_decay_bwd, ret_norm_w=ret_norm_w,
             norm2_w=norm2_w, final_norm_w=final_norm_w,
             w_in_bf=w_in.astype(BF16), w_out_bf=w_out.astype(BF16),
             w_ffn_in_bf=w_ffn_in.astype(BF16), w_ffn_out_bf=w_ffn_out.astype(BF16))
    return _trunk(x_prompt, p), _trunk(x_sample, p)
```

```python
import functools

import jax
import jax.numpy as jnp
from jax import lax
from jax.experimental import pallas as pl
from jax.experimental.pallas import tpu as pltpu

F32 = jnp.float32
BF16 = jnp.bfloat16

EPS = 1e-6
ROPE_BASE = 10000.0
RET_HEADS = 8
CONV_KERNEL = 31
CONV_PAD = CONV_KERNEL // 2
SUBLANES = 8
LANES = 128
HALO = 16
RET_CHUNK = 256
VMEM_LIMIT = 56 * 1024 * 1024


def _sigmoid(x):
    return 1.0 / (1.0 + jnp.exp(-x))


def _rms_rows(x, w):
    ms = jnp.mean(x * x, axis=-1, keepdims=True)
    return x * lax.rsqrt(ms + EPS) * w


def _dot(a, b):
    return jnp.dot(a, b, preferred_element_type=F32)


def _in_proj_kernel(x_ref, nw_ref, w_ref, cos_ref, sin_ref,
                    hg_ref, q_ref, k_ref, v_ref, g_ref, h0_ref, h1_ref,
                    *, wc, wr, heads, row_chunk):
    s = pl.program_id(0)
    tm = x_ref.shape[0]
    d = wr // heads

    @pl.when(s == 0)
    def _():
        h1_ref[...] = jnp.zeros(h1_ref.shape, h1_ref.dtype)

    def step(h_read, h_write):
        h = h_read[...]

        def proj(c0, width):
            return _dot(h, w_ref[:, c0:c0 + width])

        hg_ref[...] = proj(0, wc) * _sigmoid(proj(wc, wc))
        cos = cos_ref[...]
        sin = sin_ref[...]

        def rotary(x, hd):
            xh = x[:, hd * d:(hd + 1) * d]
            return xh * cos + pltpu.roll(xh, d // 2, axis=1) * sin

        q = proj(2 * wc, wr)
        for hd in range(heads):
            q_ref[:, hd * d:(hd + 1) * d] = rotary(q, hd).astype(q_ref.dtype)
        k = proj(2 * wc + wr, wr)
        for hd in range(heads):
            k_ref[:, hd * d:(hd + 1) * d] = rotary(k, hd) * (d ** -0.5)
        v_ref[...] = proj(2 * wc + 2 * wr, wr).astype(v_ref.dtype)
        g_ref[...] = proj(2 * wc + 3 * wr, wr)
        for c in range(tm // row_chunk):
            rows = slice(c * row_chunk, (c + 1) * row_chunk)
            h_write[rows, :] = _rms_rows(x_ref[rows, :], nw_ref[...]).astype(BF16)

    @pl.when(s % 2 == 0)
    def _():
        step(h1_ref, h0_ref)

    @pl.when(s % 2 == 1)
    def _():
        step(h0_ref, h1_ref)


def _in_proj(x2, norm_w, w_bf, cos_t, sin_t, *, seq, wc, wr, tm=256):
    T, D = x2.shape
    N = w_bf.shape[1]
    tm = min(tm, seq)
    n = T // tm
    d = wr // RET_HEADS
    per_seq = seq // tm

    def row_spec(width):
        return pl.BlockSpec((tm, width), lambda s: (jnp.maximum(s - 1, 0), 0))

    return pl.pallas_call(
        functools.partial(_in_proj_kernel, wc=wc, wr=wr, heads=RET_HEADS, row_chunk=64),
        grid=(n + 1,),
        in_specs=[
            pl.BlockSpec((tm, D), lambda s: (jnp.minimum(s, n - 1), 0)),
            pl.BlockSpec((1, D), lambda s: (0, 0)),
            pl.BlockSpec((D, N), lambda s: (0, 0), pipeline_mode=pl.Buffered(1)),
            pl.BlockSpec((tm, d), lambda s: (jnp.maximum(s - 1, 0) % per_seq, 0)),
            pl.BlockSpec((tm, d), lambda s: (jnp.maximum(s - 1, 0) % per_seq, 0)),
        ],
        out_specs=[row_spec(wc), row_spec(wr), row_spec(wr), row_spec(wr), row_spec(wr)],
        out_shape=[
            jax.ShapeDtypeStruct((T, wc), F32),
            jax.ShapeDtypeStruct((T, wr), BF16),
            jax.ShapeDtypeStruct((T, wr), F32),
            jax.ShapeDtypeStruct((T, wr), BF16),
            jax.ShapeDtypeStruct((T, wr), F32),
        ],
        scratch_shapes=[pltpu.VMEM((tm, D), BF16), pltpu.VMEM((tm, D), BF16)],
        compiler_params=pltpu.CompilerParams(
            dimension_semantics=("arbitrary",),
            vmem_limit_bytes=VMEM_LIMIT),
        name="in_proj",
    )(x2, norm_w.reshape(1, D), w_bf, cos_t, sin_t)


def _retention_kernel(lg_ref, q_ref, k_ref, v_ref, g_ref, nw_ref,
                      y_ref, kr_ref, kv_ref, r_ref, *, chunk):
    S, d = q_ref.shape
    C = chunk
    n_chunks = S // C
    h = pl.program_id(1)
    lg_f = lg_ref[0, h]
    lg_b = lg_ref[1, h]

    pos = lax.broadcasted_iota(jnp.int32, (C, d), 0).astype(F32)
    zeta_f = jnp.exp((C - 1 - pos) * lg_f)
    xi_f = jnp.exp((pos + 1.0) * lg_f)
    zeta_b = jnp.exp(pos * lg_b)
    xi_b = jnp.exp((C - pos) * lg_b)
    row = lax.broadcasted_iota(jnp.int32, (C, C), 0)
    col = lax.broadcasted_iota(jnp.int32, (C, C), 1)
    diff = (row - col).astype(F32)
    decay = jnp.exp(jnp.where(diff >= 0, diff * lg_f, -diff * lg_b))

    def prep(n, carry):
        r = pl.multiple_of(n * C, C)
        k = k_ref[pl.ds(r, C), :]
        kr_ref[pl.ds(r, C), :] = k.astype(BF16)
        kz = jnp.concatenate([k * zeta_f, k * zeta_b], axis=1).astype(BF16)
        kv_ref[n] = lax.dot_general(kz, v_ref[pl.ds(r, C), :], (((0,), (0,)), ((), ())),
                                    preferred_element_type=F32)
        return carry
    lax.fori_loop(0, n_chunks, prep, 0, unroll=8)

    dec_f = jnp.exp(C * lg_f)
    dec_b = jnp.exp(C * lg_b)

    def scan_f(n, state):
        r_ref[n, :, 0:d] = state.astype(BF16)
        return dec_f * state + kv_ref[n, 0:d, :]
    lax.fori_loop(0, n_chunks, scan_f, jnp.zeros((d, d), F32))

    def scan_b(t, state):
        n = n_chunks - 1 - t
        r_ref[n, :, d:2 * d] = state.astype(BF16)
        return dec_b * state + kv_ref[n, d:2 * d, :]
    lax.fori_loop(0, n_chunks, scan_b, jnp.zeros((d, d), F32))

    nw = nw_ref[...]

    def out(n, carry):
        r = pl.multiple_of(n * C, C)
        q = q_ref[pl.ds(r, C), :]
        s = lax.dot_general(q, kr_ref[pl.ds(r, C), :], (((1,), (1,)), ((), ())),
                            preferred_element_type=F32)
        p = (s * decay).astype(BF16)
        o = _dot(p, v_ref[pl.ds(r, C), :])
        cross = _dot(q, r_ref[n])
        o = o + cross[:, 0:d] * xi_f + cross[:, d:2 * d] * xi_b
        o = _rms_rows(o, nw)
        g = g_ref[pl.ds(r, C), :]
        y_ref[pl.ds(r, C), :] = (g * _sigmoid(g) * o).astype(y_ref.dtype)
        return carry
    lax.fori_loop(0, n_chunks, out, 0, unroll=8)


def _retention(q3, k3, v3, g3, lg, norm_w):
    B, S, width = q3.shape
    H = RET_HEADS
    d = width // H
    C = min(RET_CHUNK, S)

    def head_spec():
        return pl.BlockSpec((None, S, d), lambda b, h: (b, 0, h))

    return pl.pallas_call(
        functools.partial(_retention_kernel, chunk=C),
        grid=(B, H),
        in_specs=[
            pl.BlockSpec(memory_space=pltpu.SMEM),
            head_spec(), head_spec(), head_spec(), head_spec(),
            pl.BlockSpec((1, d), lambda b, h: (0, h)),
        ],
        out_specs=head_spec(),
        out_shape=jax.ShapeDtypeStruct((B, S, width), BF16),
        scratch_shapes=[
            pltpu.VMEM((S, d), BF16),
            pltpu.VMEM((S // C, 2 * d, d), F32),
            pltpu.VMEM((S // C, d, 2 * d), BF16),
        ],
        compiler_params=pltpu.CompilerParams(
            dimension_semantics=("parallel", "arbitrary"),
            vmem_limit_bytes=VMEM_LIMIT),
        name="retention",
    )(lg, q3, k3, v3, g3, norm_w.reshape(1, width))


def _conv_kernel(x_ref, xp_ref, xn_ref, cw_ref, cb_ref, lw_ref, lb_ref,
                 y_ref, xs_ref, cv_ref, *, row_block, norm_block):
    tm, W = x_ref.shape
    n_lt = W // LANES
    i = pl.program_id(1)
    last = pl.num_programs(1) - 1
    off0 = HALO - CONV_PAD

    for t in range(n_lt):
        ls = slice(t * LANES, (t + 1) * LANES)
        xs_ref[t, HALO:HALO + tm, :] = x_ref[:, ls]
        xs_ref[t, 0:HALO, :] = jnp.where(i > 0, xp_ref[:, ls], 0.0)
        xs_ref[t, HALO + tm:2 * HALO + tm, :] = jnp.where(i < last, xn_ref[:, ls], 0.0)

    reps = row_block // SUBLANES

    def conv_tile(idx, carry):
        rb = idx // n_lt
        t = idx % n_lt
        r0 = rb * row_block
        acc = jnp.concatenate([cb_ref[t]] * reps, axis=0)
        for k in range(CONV_KERNEL):
            wk = jnp.concatenate([cw_ref[t, k]] * reps, axis=0)
            acc = acc + xs_ref[t, pl.ds(r0 + (off0 + k), row_block), :] * wk
        cv_ref[t, pl.ds(pl.multiple_of(r0, row_block), row_block), :] = acc
        return carry
    lax.fori_loop(0, (tm // row_block) * n_lt, conv_tile, 0)

    def norm_rows(rb, carry):
        r0 = pl.multiple_of(rb * norm_block, norm_block)
        c = jnp.concatenate([cv_ref[t, pl.ds(r0, norm_block), :] for t in range(n_lt)], axis=1)
        mu = jnp.mean(c, axis=-1, keepdims=True)
        cc = c - mu
        var = jnp.mean(cc * cc, axis=-1, keepdims=True)
        z = cc * lax.rsqrt(var + EPS) * lw_ref[...] + lb_ref[...]
        y_ref[pl.ds(r0, norm_block), :] = (z * _sigmoid(z)).astype(y_ref.dtype)
        return carry
    lax.fori_loop(0, tm // norm_block, norm_rows, 0, unroll=4)


def _conv_mixer(x3, conv_w, conv_b, ln_w, ln_b, *, tm=256):
    B, S, W = x3.shape
    n_lt = W // LANES
    tm = min(tm, S)
    hb = tm // HALO
    n_halo = S // HALO
    cw8 = jnp.broadcast_to(conv_w.reshape(CONV_KERNEL, n_lt, 1, LANES).transpose(1, 0, 2, 3),
                           (n_lt, CONV_KERNEL, SUBLANES, LANES))
    cb8 = jnp.broadcast_to(conv_b.reshape(n_lt, 1, LANES), (n_lt, SUBLANES, LANES))
    return pl.pallas_call(
        functools.partial(_conv_kernel, row_block=min(128, tm), norm_block=32),
        grid=(B, S // tm),
        in_specs=[
            pl.BlockSpec((None, tm, W), lambda b, i: (b, i, 0)),
            pl.BlockSpec((None, HALO, W), lambda b, i: (b, jnp.maximum(i * hb - 1, 0), 0)),
            pl.BlockSpec((None, HALO, W), lambda b, i: (b, jnp.minimum((i + 1) * hb, n_halo - 1), 0)),
            pl.BlockSpec((n_lt, CONV_KERNEL, SUBLANES, LANES), lambda b, i: (0, 0, 0, 0)),
            pl.BlockSpec((n_lt, SUBLANES, LANES), lambda b, i: (0, 0, 0)),
            pl.BlockSpec((1, W), lambda b, i: (0, 0)),
            pl.BlockSpec((1, W), lambda b, i: (0, 0)),
        ],
        out_specs=pl.BlockSpec((None, tm, W), lambda b, i: (b, i, 0)),
        out_shape=jax.ShapeDtypeStruct((B, S, W), BF16),
        scratch_shapes=[pltpu.VMEM((n_lt, tm + 2 * HALO, LANES), F32),
                        pltpu.VMEM((n_lt, tm, LANES), F32)],
        compiler_params=pltpu.CompilerParams(
            dimension_semantics=("parallel", "arbitrary"),
            vmem_limit_bytes=VMEM_LIMIT),
        name="conv_mixer",
    )(x3, x3, x3, cw8, cb8, ln_w.reshape(1, W), ln_b.reshape(1, W))


def _out_proj_kernel(x_ref, yc_ref, yr_ref, w_ref, o_ref):
    wc = yc_ref.shape[1]
    acc = _dot(yc_ref[...], w_ref[0:wc, :])
    acc = acc + _dot(yr_ref[...], w_ref[wc:, :])
    o_ref[...] = x_ref[...] + acc


def _out_proj(x2, yc2, yr2, w_bf, *, tm=512):
    T, D = x2.shape
    wc, wr = yc2.shape[1], yr2.shape[1]
    tm = min(tm, T)
    return pl.pallas_call(
        _out_proj_kernel,
        grid=(T // tm,),
        in_specs=[
            pl.BlockSpec((tm, D), lambda i: (i, 0)),
            pl.BlockSpec((tm, wc), lambda i: (i, 0)),
            pl.BlockSpec((tm, wr), lambda i: (i, 0)),
            pl.BlockSpec((wc + wr, D), lambda i: (0, 0)),
        ],
        out_specs=pl.BlockSpec((tm, D), lambda i: (i, 0)),
        out_shape=jax.ShapeDtypeStruct((T, D), F32),
        compiler_params=pltpu.CompilerParams(
            dimension_semantics=("parallel",),
            vmem_limit_bytes=VMEM_LIMIT),
        name="out_proj",
    )(x2, yc2, yr2, w_bf)


def _ffn_kernel(x_ref, n2_ref, wg_ref, wu_ref, wo_ref, fn_ref, o_ref, h_ref, acc_ref,
                *, row_chunk, final_norm):
    tm = x_ref.shape[0]
    tf = wg_ref.shape[1]
    half = tf // 2
    i = pl.program_id(0)
    j = pl.program_id(1)
    n = pl.num_programs(0) - 1
    nf = pl.num_programs(1)
    chunks = [slice(c * row_chunk, (c + 1) * row_chunk) for c in range(tm // row_chunk)]

    def normalise():
        for rows in chunks:
            h_ref[rows, :] = _rms_rows(x_ref[rows, :], n2_ref[...]).astype(BF16)

    def start_residual():
        for rows in chunks:
            acc_ref[rows, :] = x_ref[rows, :]

    def finish():
        for rows in chunks:
            y = acc_ref[rows, :]
            o_ref[rows, :] = _rms_rows(y, fn_ref[...]) if final_norm else y

    def swiglu_chunk(after_up=None):
        h = h_ref[...]
        g0 = _dot(h, wg_ref[:, 0:half])
        u0 = _dot(h, wu_ref[:, 0:half])
        g1 = _dot(h, wg_ref[:, half:tf])
        u1 = _dot(h, wu_ref[:, half:tf])
        if after_up is not None:
            after_up()
        a0 = (g0 * _sigmoid(g0) * u0).astype(BF16)
        a1 = (g1 * _sigmoid(g1) * u1).astype(BF16)
        acc_ref[...] += _dot(a0, wo_ref[0:half, :]) + _dot(a1, wo_ref[half:tf, :])

    first = j == 0
    inner = i < n

    @pl.when(first & (i == 0))
    def _():
        normalise()
        start_residual()
        swiglu_chunk()

    @pl.when(first & (i > 0) & inner)
    def _():
        finish()
        start_residual()
        swiglu_chunk()

    @pl.when(first & (i == n))
    def _():
        finish()

    @pl.when((j > 0) & (j < nf - 1) & inner)
    def _():
        swiglu_chunk()

    @pl.when((j == nf - 1) & inner)
    def _():
        swiglu_chunk(after_up=normalise)


def _ffn(x2, norm_w, w_in_bf, w_out_bf, final_w, *, final_norm, tm=512, tf=512):
    T, D = x2.shape
    F = w_out_bf.shape[0]
    tm = min(tm, T)
    n = T // tm
    nf = F // tf
    assert nf >= 2 and F % tf == 0 and T % tm == 0

    def chunk(i, j):
        return jnp.where(i < n, j, nf - 1)

    return pl.pallas_call(
        functools.partial(_ffn_kernel, row_chunk=64, final_norm=final_norm),
        grid=(n + 1, nf),
        in_specs=[
            pl.BlockSpec((tm, D), lambda i, j: (jnp.minimum(i + jnp.minimum(j, 1), n - 1), 0)),
            pl.BlockSpec((1, D), lambda i, j: (0, 0)),
            pl.BlockSpec((D, tf), lambda i, j: (0, chunk(i, j))),
            pl.BlockSpec((D, tf), lambda i, j: (0, chunk(i, j) + nf)),
            pl.BlockSpec((tf, D), lambda i, j: (chunk(i, j), 0)),
            pl.BlockSpec((1, D), lambda i, j: (0, 0)),
        ],
        out_specs=pl.BlockSpec((tm, D), lambda i, j: (jnp.maximum(i - 1, 0), 0)),
        out_shape=jax.ShapeDtypeStruct((T, D), F32),
        scratch_shapes=[pltpu.VMEM((tm, D), BF16), pltpu.VMEM((tm, D), F32)],
        compiler_params=pltpu.CompilerParams(
            dimension_semantics=("arbitrary", "arbitrary"),
            vmem_limit_bytes=VMEM_LIMIT),
        name="ffn",
    )(x2, norm_w.reshape(1, D), w_in_bf, w_in_bf, w_out_bf, final_w.reshape(1, D))


def _rope_tables(S, d):
    inv_freq = ROPE_BASE ** (-jnp.arange(0, d, 2, dtype=F32) / d)
    ang = jnp.arange(S, dtype=F32)[:, None] * inv_freq[None, :]
    cos, sin = jnp.cos(ang), jnp.sin(ang)
    return jnp.concatenate([cos, cos], axis=1), jnp.concatenate([-sin, sin], axis=1)


def _trunk(x, p):
    B, S, D = x.shape
    depth = p["w_in_bf"].shape[0]
    wc = p["conv_w"].shape[-1]
    wr = p["ret_norm_w"].shape[-1]
    cos_t, sin_t = _rope_tables(S, wr // RET_HEADS)
    x2 = x.reshape(B * S, D)
    for l in range(depth):
        last = l == depth - 1
        hg, q, k, v, g = _in_proj(x2, p["norm1_w"][l], p["w_in_bf"][l], cos_t, sin_t,
                                  seq=S, wc=wc, wr=wr)
        lg = jnp.stack([p["ret_log_decay_fwd"][l], p["ret_log_decay_bwd"][l]]).astype(F32)
        yr = _retention(q.reshape(B, S, wr), k.reshape(B, S, wr), v.reshape(B, S, wr),
                        g.reshape(B, S, wr), lg, p["ret_norm_w"][l])
        yc = _conv_mixer(hg.reshape(B, S, wc), p["conv_w"][l], p["conv_b"][l],
                         p["conv_ln_w"][l], p["conv_ln_b"][l])
        x2 = _out_proj(x2, yc.reshape(B * S, wc), yr.reshape(B * S, wr), p["w_out_bf"][l])
        x2 = _ffn(x2, p["norm2_w"][l], p["w_ffn_in_bf"][l], p["w_ffn_out_bf"][l],
                  p["final_norm_w"], final_norm=last)
    return x2.reshape(B, S, D)


def kernel(x_prompt, x_sample, norm1_w, w_in, conv_w, conv_b, conv_ln_w, conv_ln_b,
           ret_log_decay_fwd, ret_log_decay_bwd, ret_norm_w, w_out,
           norm2_w, w_ffn_in, w_ffn_out, final_norm_w):
    p = dict(norm1_w=norm1_w, conv_w=conv_w, conv_b=conv_b, conv_ln_w=conv_ln_w,
             conv_ln_b=conv_ln_b, ret_log_decay_fwd=ret_log_decay_fwd,
             ret_log_decay_bwd=ret_log_decay_bwd, ret_norm_w=ret_norm_w,
             norm2_w=norm2_w, final_norm_w=final_norm_w,
             w_in_bf=w_in.astype(BF16), w_out_bf=w_out.astype(BF16),
             w_ffn_in_bf=w_ffn_in.astype(BF16), w_ffn_out_bf=w_ffn_out.astype(BF16))
    return _trunk(x_prompt, p), _trunk(x_sample, p)
```

```python
import functools

import jax
import jax.numpy as jnp
from jax import lax
from jax.experimental import pallas as pl
from jax.experimental.pallas import tpu as pltpu

F32 = jnp.float32
BF16 = jnp.bfloat16

EPS = 1e-6
ROPE_BASE = 10000.0
RET_HEADS = 8
CONV_KERNEL = 31
CONV_PAD = CONV_KERNEL // 2
SUBLANES = 8
LANES = 128
HALO = 16
RET_CHUNK = 256
VMEM_LIMIT = 56 * 1024 * 1024


def _sigmoid(x):
    return 1.0 / (1.0 + jnp.exp(-x))


def _rms_rows(x, w):
    ms = jnp.mean(x * x, axis=-1, keepdims=True)
    return x * lax.rsqrt(ms + EPS) * w


def _dot(a, b):
    return jnp.dot(a, b, preferred_element_type=F32)


def _in_proj_kernel(x_ref, nw_ref, w_ref, cos_ref, sin_ref,
                    hg_ref, q_ref, k_ref, v_ref, g_ref, h0_ref, h1_ref,
                    *, wc, wr, heads, row_chunk):
    s = pl.program_id(0)
    tm = x_ref.shape[0]
    d = wr // heads

    @pl.when(s == 0)
    def _():
        h1_ref[...] = jnp.zeros(h1_ref.shape, h1_ref.dtype)

    def step(h_read, h_write):
        h = h_read[...]

        def proj(c0, width):
            return _dot(h, w_ref[:, c0:c0 + width])

        hg_ref[...] = proj(0, wc) * _sigmoid(proj(wc, wc))
        cos = cos_ref[...]
        sin = sin_ref[...]

        def rotary(x, hd):
            xh = x[:, hd * d:(hd + 1) * d]
            return xh * cos + pltpu.roll(xh, d // 2, axis=1) * sin

        q = proj(2 * wc, wr)
        for hd in range(heads):
            q_ref[:, hd * d:(hd + 1) * d] = rotary(q, hd).astype(q_ref.dtype)
        k = proj(2 * wc + wr, wr)
        for hd in range(heads):
            k_ref[:, hd * d:(hd + 1) * d] = rotary(k, hd) * (d ** -0.5)
        v_ref[...] = proj(2 * wc + 2 * wr, wr).astype(v_ref.dtype)
        g_ref[...] = proj(2 * wc + 3 * wr, wr)
        for c in range(tm // row_chunk):
            rows = slice(c * row_chunk, (c + 1) * row_chunk)
            h_write[rows, :] = _rms_rows(x_ref[rows, :], nw_ref[...]).astype(BF16)

    @pl.when(s % 2 == 0)
    def _():
        step(h1_ref, h0_ref)

    @pl.when(s % 2 == 1)
    def _():
        step(h0_ref, h1_ref)


def _in_proj(x2, norm_w, w_bf, cos_t, sin_t, *, seq, wc, wr, tm=256):
    T, D = x2.shape
    N = w_bf.shape[1]
    tm = min(tm, seq)
    n = T // tm
    d = wr // RET_HEADS
    per_seq = seq // tm

    def row_spec(width):
        return pl.BlockSpec((tm, width), lambda s: (jnp.maximum(s - 1, 0), 0))

    return pl.pallas_call(
        functools.partial(_in_proj_kernel, wc=wc, wr=wr, heads=RET_HEADS, row_chunk=64),
        grid=(n + 1,),
        in_specs=[
            pl.BlockSpec((tm, D), lambda s: (jnp.minimum(s, n - 1), 0)),
            pl.BlockSpec((1, D), lambda s: (0, 0)),
            pl.BlockSpec((D, N), lambda s: (0, 0), pipeline_mode=pl.Buffered(1)),
            pl.BlockSpec((tm, d), lambda s: (jnp.maximum(s - 1, 0) % per_seq, 0)),
            pl.BlockSpec((tm, d), lambda s: (jnp.maximum(s - 1, 0) % per_seq, 0)),
        ],
        out_specs=[row_spec(wc), row_spec(wr), row_spec(wr), row_spec(wr), row_spec(wr)],
        out_shape=[
            jax.ShapeDtypeStruct((T, wc), F32),
            jax.ShapeDtypeStruct((T, wr), BF16),
            jax.ShapeDtypeStruct((T, wr), F32),
            jax.ShapeDtypeStruct((T, wr), BF16),
            jax.ShapeDtypeStruct((T, wr), F32),
        ],
        scratch_shapes=[pltpu.VMEM((tm, D), BF16), pltpu.VMEM((tm, D), BF16)],
        compiler_params=pltpu.CompilerParams(
            dimension_semantics=("arbitrary",),
            vmem_limit_bytes=VMEM_LIMIT),
        name="in_proj",
    )(x2, norm_w.reshape(1, D), w_bf, cos_t, sin_t)


def _retention_kernel(lg_ref, q_ref, k_ref, v_ref, g_ref, nw_ref,
                      y_ref, kr_ref, kv_ref, r_ref, *, chunk):
    S, d = q_ref.shape
    C = chunk
    n_chunks = S // C
    h = pl.program_id(1)
    lg_f = lg_ref[0, h]
    lg_b = lg_ref[1, h]

    pos = lax.broadcasted_iota(jnp.int32, (C, d), 0).astype(F32)
    zeta_f = jnp.exp((C - 1 - pos) * lg_f)
    xi_f = jnp.exp((pos + 1.0) * lg_f)
    zeta_b = jnp.exp(pos * lg_b)
    xi_b = jnp.exp((C - pos) * lg_b)
    row = lax.broadcasted_iota(jnp.int32, (C, C), 0)
    col = lax.broadcasted_iota(jnp.int32, (C, C), 1)
    diff = (row - col).astype(F32)
    decay = jnp.exp(jnp.where(diff >= 0, diff * lg_f, -diff * lg_b))

    def prep(n, carry):
        r = pl.multiple_of(n * C, C)
        k = k_ref[pl.ds(r, C), :]
        kr_ref[pl.ds(r, C), :] = k.astype(BF16)
        kz = jnp.concatenate([k * zeta_f, k * zeta_b], axis=1).astype(BF16)
        kv_ref[n] = lax.dot_general(kz, v_ref[pl.ds(r, C), :], (((0,), (0,)), ((), ())),
                                    preferred_element_type=F32)
        return carry
    lax.fori_loop(0, n_chunks, prep, 0, unroll=8)

    dec_f = jnp.exp(C * lg_f)
    dec_b = jnp.exp(C * lg_b)

    def scan_f(n, state):
        r_ref[n, :, 0:d] = state.astype(BF16)
        return dec_f * state + kv_ref[n, 0:d, :]
    lax.fori_loop(0, n_chunks, scan_f, jnp.zeros((d, d), F32))

    def scan_b(t, state):
        n = n_chunks - 1 - t
        r_ref[n, :, d:2 * d] = state.astype(BF16)
        return dec_b * state + kv_ref[n, d:2 * d, :]
    lax.fori_loop(0, n_chunks, scan_b, jnp.zeros((d, d), F32))

    nw = nw_ref[...]

    def out(n, carry):
        r = pl.multiple_of(n * C, C)
        q = q_ref[pl.ds(r, C), :]
        s = lax.dot_general(q, kr_ref[pl.ds(r, C), :], (((1,), (1,)), ((), ())),
                            preferred_element_type=F32)
        p = (s * decay).astype(BF16)
        o = _dot(p, v_ref[pl.ds(r, C), :])
        cross = _dot(q, r_ref[n])
        o = o + cross[:, 0:d] * xi_f + cross[:, d:2 * d] * xi_b
        o = _rms_rows(o, nw)
        g = g_ref[pl.ds(r, C), :]
        y_ref[pl.ds(r, C), :] = (g * _sigmoid(g) * o).astype(y_ref.dtype)
        return carry
    lax.fori_loop(0, n_chunks, out, 0, unroll=8)


def _retention(q3, k3, v3, g3, lg, norm_w):
    B, S, width = q3.shape
    H = RET_HEADS
    d = width // H
    C = min(RET_CHUNK, S)

    def head_spec():
        return pl.BlockSpec((None, S, d), lambda b, h: (b, 0, h))

    return pl.pallas_call(
        functools.partial(_retention_kernel, chunk=C),
        grid=(B, H),
        in_specs=[
            pl.BlockSpec(memory_space=pltpu.SMEM),
            head_spec(), head_spec(), head_spec(), head_spec(),
            pl.BlockSpec((1, d), lambda b, h: (0, h)),
        ],
        out_specs=head_spec(),
        out_shape=jax.ShapeDtypeStruct((B, S, width), BF16),
        scratch_shapes=[
            pltpu.VMEM((S, d), BF16),
            pltpu.VMEM((S // C, 2 * d, d), F32),
            pltpu.VMEM((S // C, d, 2 * d), BF16),
        ],
        compiler_params=pltpu.CompilerParams(
            dimension_semantics=("parallel", "arbitrary"),
            vmem_limit_bytes=VMEM_LIMIT),
        name="retention",
    )(lg, q3, k3, v3, g3, norm_w.reshape(1, width))


def _conv_kernel(x_ref, xp_ref, xn_ref, cw_ref, cb_ref, lw_ref, lb_ref,
                 y_ref, xs_ref, cv_ref, *, row_block, norm_block):
    tm, W = x_ref.shape
    n_lt = W // LANES
    i = pl.program_id(1)
    last = pl.num_programs(1) - 1
    off0 = HALO - CONV_PAD

    for t in range(n_lt):
        ls = slice(t * LANES, (t + 1) * LANES)
        xs_ref[t, HALO:HALO + tm, :] = x_ref[:, ls]
        xs_ref[t, 0:HALO, :] = jnp.where(i > 0, xp_ref[:, ls], 0.0)
        xs_ref[t, HALO + tm:2 * HALO + tm, :] = jnp.where(i < last, xn_ref[:, ls], 0.0)

    reps = row_block // SUBLANES

    def conv_tile(idx, carry):
        rb = idx // n_lt
        t = idx % n_lt
        r0 = rb * row_block
        acc = jnp.concatenate([cb_ref[t]] * reps, axis=0)
        for k in range(CONV_KERNEL):
            wk = jnp.concatenate([cw_ref[t, k]] * reps, axis=0)
            acc = acc + xs_ref[t, pl.ds(r0 + (off0 + k), row_block), :] * wk
        cv_ref[t, pl.ds(pl.multiple_of(r0, row_block), row_block), :] = acc
        return carry
    lax.fori_loop(0, (tm // row_block) * n_lt, conv_tile, 0)

    def norm_rows(rb, carry):
        r0 = pl.multiple_of(rb * norm_block, norm_block)
        c = jnp.concatenate([cv_ref[t, pl.ds(r0, norm_block), :] for t in range(n_lt)], axis=1)
        mu = jnp.mean(c, axis=-1, keepdims=True)
        cc = c - mu
        var = jnp.mean(cc * cc, axis=-1, keepdims=True)
        z = cc * lax.rsqrt(var + EPS) * lw_ref[...] + lb_ref[...]
        y_ref[pl.ds(r0, norm_block), :] = (z * _sigmoid(z)).astype(y_ref.dtype)
        return carry
    lax.fori_loop(0, tm // norm_block, norm_rows, 0, unroll=4)


def _conv_mixer(x3, conv_w, conv_b, ln_w, ln_b, *, tm=256):
    B, S, W = x3.shape
    n_lt = W // LANES
    tm = min(tm, S)
    hb = tm // HALO
    n_halo = S // HALO
    cw8 = jnp.broadcast_to(conv_w.reshape(CONV_KERNEL, n_lt, 1, LANES).transpose(1, 0, 2, 3),
                           (n_lt, CONV_KERNEL, SUBLANES, LANES))
    cb8 = jnp.broadcast_to(conv_b.reshape(n_lt, 1, LANES), (n_lt, SUBLANES, LANES))
    return pl.pallas_call(
        functools.partial(_conv_kernel, row_block=min(128, tm), norm_block=32),
        grid=(B, S // tm),
        in_specs=[
            pl.BlockSpec((None, tm, W), lambda b, i: (b, i, 0)),
            pl.BlockSpec((None, HALO, W), lambda b, i: (b, jnp.maximum(i * hb - 1, 0), 0)),
            pl.BlockSpec((None, HALO, W), lambda b, i: (b, jnp.minimum((i + 1) * hb, n_halo - 1), 0)),
            pl.BlockSpec((n_lt, CONV_KERNEL, SUBLANES, LANES), lambda b, i: (0, 0, 0, 0)),
            pl.BlockSpec((n_lt, SUBLANES, LANES), lambda b, i: (0, 0, 0)),
            pl.BlockSpec((1, W), lambda b, i: (0, 0)),
            pl.BlockSpec((1, W), lambda b, i: (0, 0)),
        ],
        out_specs=pl.BlockSpec((None, tm, W), lambda b, i: (b, i, 0)),
        out_shape=jax.ShapeDtypeStruct((B, S, W), BF16),
        scratch_shapes=[pltpu.VMEM((n_lt, tm + 2 * HALO, LANES), F32),
                        pltpu.VMEM((n_lt, tm, LANES), F32)],
        compiler_params=pltpu.CompilerParams(
            dimension_semantics=("parallel", "arbitrary"),
            vmem_limit_bytes=VMEM_LIMIT),
        name="conv_mixer",
    )(x3, x3, x3, cw8, cb8, ln_w.reshape(1, W), ln_b.reshape(1, W))


def _out_proj_kernel(x_ref, yc_ref, yr_ref, w_ref, n2_ref, o_ref, h2_ref, *, row_chunk):
    tm = x_ref.shape[0]
    wc = yc_ref.shape[1]
    half = tm // 2
    for part in range(2):
        rows = slice(part * half, (part + 1) * half)
        acc = _dot(yc_ref[rows, :], w_ref[0:wc, :]) + _dot(yr_ref[rows, :], w_ref[wc:, :])
        for c in range(half // row_chunk):
            sub = slice(part * half + c * row_chunk, part * half + (c + 1) * row_chunk)
            x1 = x_ref[sub, :] + acc[c * row_chunk:(c + 1) * row_chunk, :]
            o_ref[sub, :] = x1
            h2_ref[sub, :] = _rms_rows(x1, n2_ref[...]).astype(h2_ref.dtype)


def _out_proj(x2, yc2, yr2, w_bf, norm2_w, *, tm=512):
    T, D = x2.shape
    wc, wr = yc2.shape[1], yr2.shape[1]
    tm = min(tm, T)
    return pl.pallas_call(
        functools.partial(_out_proj_kernel, row_chunk=64),
        grid=(T // tm,),
        in_specs=[
            pl.BlockSpec((tm, D), lambda i: (i, 0)),
            pl.BlockSpec((tm, wc), lambda i: (i, 0)),
            pl.BlockSpec((tm, wr), lambda i: (i, 0)),
            pl.BlockSpec((wc + wr, D), lambda i: (0, 0)),
            pl.BlockSpec((1, D), lambda i: (0, 0)),
        ],
        out_specs=[pl.BlockSpec((tm, D), lambda i: (i, 0)),
                   pl.BlockSpec((tm, D), lambda i: (i, 0))],
        out_shape=[jax.ShapeDtypeStruct((T, D), F32), jax.ShapeDtypeStruct((T, D), BF16)],
        compiler_params=pltpu.CompilerParams(
            dimension_semantics=("parallel",),
            vmem_limit_bytes=VMEM_LIMIT),
        name="out_proj",
    )(x2, yc2, yr2, w_bf, norm2_w.reshape(1, D))


def _ffn_kernel(h_ref, x_hbm, wg_ref, wu_ref, wo_ref, fn_ref, o_ref, sem,
                *, row_chunk, final_norm):
    tm = h_ref.shape[0]
    tf = wg_ref.shape[1]
    half = tf // 2
    i = pl.program_id(0)
    j = pl.program_id(1)

    def residual_copy():
        return pltpu.make_async_copy(x_hbm.at[pl.ds(i * tm, tm), :], o_ref, sem)

    def hidden():
        h = h_ref[...]
        g0 = _dot(h, wg_ref[:, 0:half])
        u0 = _dot(h, wu_ref[:, 0:half])
        g1 = _dot(h, wg_ref[:, half:tf])
        u1 = _dot(h, wu_ref[:, half:tf])
        a0 = (g0 * _sigmoid(g0) * u0).astype(BF16)
        a1 = (g1 * _sigmoid(g1) * u1).astype(BF16)
        return a0, a1

    def accumulate(a0, a1):
        o_ref[...] += _dot(a0, wo_ref[0:half, :]) + _dot(a1, wo_ref[half:tf, :])

    @pl.when(j == 0)
    def _():
        residual_copy().start()
        a0, a1 = hidden()
        residual_copy().wait()
        accumulate(a0, a1)

    @pl.when(j > 0)
    def _():
        accumulate(*hidden())

    if final_norm:
        @pl.when(j == pl.num_programs(1) - 1)
        def _():
            for c in range(tm // row_chunk):
                rows = slice(c * row_chunk, (c + 1) * row_chunk)
                o_ref[rows, :] = _rms_rows(o_ref[rows, :], fn_ref[...])


def _ffn(h2, x1, w_in_bf, w_out_bf, final_w, *, final_norm, tm=1024, tf=512):
    T, D = x1.shape
    F = w_out_bf.shape[0]
    tm = min(tm, T)
    nf = F // tf
    assert F % tf == 0 and T % tm == 0
    return pl.pallas_call(
        functools.partial(_ffn_kernel, row_chunk=64, final_norm=final_norm),
        grid=(T // tm, nf),
        in_specs=[
            pl.BlockSpec((tm, D), lambda i, j: (i, 0)),
            pl.BlockSpec(memory_space=pl.ANY),
            pl.BlockSpec((D, tf), lambda i, j: (0, j)),
            pl.BlockSpec((D, tf), lambda i, j: (0, j + nf)),
            pl.BlockSpec((tf, D), lambda i, j: (j, 0)),
            pl.BlockSpec((1, D), lambda i, j: (0, 0)),
        ],
        out_specs=pl.BlockSpec((tm, D), lambda i, j: (i, 0)),
        out_shape=jax.ShapeDtypeStruct((T, D), F32),
        scratch_shapes=[pltpu.SemaphoreType.DMA],
        compiler_params=pltpu.CompilerParams(
            dimension_semantics=("parallel", "arbitrary"),
            vmem_limit_bytes=VMEM_LIMIT),
        name="ffn",
    )(h2, x1, w_in_bf, w_in_bf, w_out_bf, final_w.reshape(1, D))


def _rope_tables(S, d):
    inv_freq = ROPE_BASE ** (-jnp.arange(0, d, 2, dtype=F32) / d)
    ang = jnp.arange(S, dtype=F32)[:, None] * inv_freq[None, :]
    cos, sin = jnp.cos(ang), jnp.sin(ang)
    return jnp.concatenate([cos, cos], axis=1), jnp.concatenate([-sin, sin], axis=1)


def _trunk(x, p):
    B, S, D = x.shape
    depth = p["w_in_bf"].shape[0]
    wc = p["conv_w"].shape[-1]
    wr = p["ret_norm_w"].shape[-1]
    cos_t, sin_t = _rope_tables(S, wr // RET_HEADS)
    x2 = x.reshape(B * S, D)
    for l in range(depth):
        last = l == depth - 1
        hg, q, k, v, g = _in_proj(x2, p["norm1_w"][l], p["w_in_bf"][l], cos_t, sin_t,
                                  seq=S, wc=wc, wr=wr)
        lg = jnp.stack([p["ret_log_decay_fwd"][l], p["ret_log_decay_bwd"][l]]).astype(F32)
        yr = _retention(q.reshape(B, S, wr), k.reshape(B, S, wr), v.reshape(B, S, wr),
                        g.reshape(B, S, wr), lg, p["ret_norm_w"][l])
        yc = _conv_mixer(hg.reshape(B, S, wc), p["conv_w"][l], p["conv_b"][l],
                         p["conv_ln_w"][l], p["conv_ln_b"][l])
        x1, h2 = _out_proj(x2, yc.reshape(B * S, wc), yr.reshape(B * S, wr), p["w_out_bf"][l],
                           p["norm2_w"][l])
        x2 = _ffn(h2, x1, p["w_ffn_in_bf"][l], p["w_ffn_out_bf"][l],
                  p["final_norm_w"], final_norm=last)
    return x2.reshape(B, S, D)


def kernel(x_prompt, x_sample, norm1_w, w_in, conv_w, conv_b, conv_ln_w, conv_ln_b,
           ret_log_decay_fwd, ret_log_decay_bwd, ret_norm_w, w_out,
           norm2_w, w_ffn_in, w_ffn_out, final_norm_w):
    p = dict(norm1_w=norm1_w, conv_w=conv_w, conv_b=conv_b, conv_ln_w=conv_ln_w,
             conv_ln_b=conv_ln_b, ret_log_decay_fwd=ret_log_decay_fwd,
             ret_log_decay_bwd=ret_log_decay_bwd, ret_norm_w=ret_norm_w,
             norm2_w=norm2_w, final_norm_w=final_norm_w,
             w_in_bf=w_in.astype(BF16), w_out_bf=w_out.astype(BF16),
             w_ffn_in_bf=w_ffn_in.astype(BF16), w_ffn_out_bf=w_ffn_out.astype(BF16))
    return _trunk(x_prompt, p), _trunk(x_sample, p)
```

```python
import functools

import jax
import jax.numpy as jnp
from jax import lax
from jax.experimental import pallas as pl
from jax.experimental.pallas import tpu as pltpu

F32 = jnp.float32
BF16 = jnp.bfloat16

EPS = 1e-6
ROPE_BASE = 10000.0
RET_HEADS = 8
CONV_KERNEL = 31
CONV_PAD = CONV_KERNEL // 2
SUBLANES = 8
LANES = 128
HALO = 16
RET_CHUNK = 256
VMEM_LIMIT = 56 * 1024 * 1024


def _sigmoid(x):
    return 1.0 / (1.0 + jnp.exp(-x))


def _rms_rows(x, w):
    ms = jnp.mean(x * x, axis=-1, keepdims=True)
    return x * lax.rsqrt(ms + EPS) * w


def _dot(a, b):
    return jnp.dot(a, b, preferred_element_type=F32)


def _in_proj_kernel(x_ref, nw_ref, w_ref, cos_ref, sin_ref,
                    hg_ref, q_ref, k_ref, v_ref, g_ref, h0_ref, h1_ref,
                    *, wc, wr, heads, row_chunk):
    s = pl.program_id(0)
    tm = x_ref.shape[0]
    d = wr // heads

    @pl.when(s == 0)
    def _():
        h1_ref[...] = jnp.zeros(h1_ref.shape, h1_ref.dtype)

    def step(h_read, h_write):
        h = h_read[...]

        def proj(c0, width):
            return _dot(h, w_ref[:, c0:c0 + width])

        hg_ref[...] = proj(0, wc) * _sigmoid(proj(wc, wc))
        cos = cos_ref[...]
        sin = sin_ref[...]

        def rotary(x, hd):
            xh = x[:, hd * d:(hd + 1) * d]
            return xh * cos + pltpu.roll(xh, d // 2, axis=1) * sin

        q = proj(2 * wc, wr)
        for hd in range(heads):
            q_ref[:, hd * d:(hd + 1) * d] = rotary(q, hd).astype(q_ref.dtype)
        k = proj(2 * wc + wr, wr)
        for hd in range(heads):
            k_ref[:, hd * d:(hd + 1) * d] = rotary(k, hd) * (d ** -0.5)
        v_ref[...] = proj(2 * wc + 2 * wr, wr).astype(v_ref.dtype)
        g_ref[...] = proj(2 * wc + 3 * wr, wr)
        for c in range(tm // row_chunk):
            rows = slice(c * row_chunk, (c + 1) * row_chunk)
            h_write[rows, :] = _rms_rows(x_ref[rows, :], nw_ref[...]).astype(BF16)

    @pl.when(s % 2 == 0)
    def _():
        step(h1_ref, h0_ref)

    @pl.when(s % 2 == 1)
    def _():
        step(h0_ref, h1_ref)


def _in_proj(x2, norm_w, w_bf, cos_t, sin_t, *, seq, wc, wr, tm=256):
    T, D = x2.shape
    N = w_bf.shape[1]
    tm = min(tm, seq)
    n = T // tm
    d = wr // RET_HEADS
    per_seq = seq // tm

    def row_spec(width):
        return pl.BlockSpec((tm, width), lambda s: (jnp.maximum(s - 1, 0), 0))

    return pl.pallas_call(
        functools.partial(_in_proj_kernel, wc=wc, wr=wr, heads=RET_HEADS, row_chunk=64),
        grid=(n + 1,),
        in_specs=[
            pl.BlockSpec((tm, D), lambda s: (jnp.minimum(s, n - 1), 0)),
            pl.BlockSpec((1, D), lambda s: (0, 0)),
            pl.BlockSpec((D, N), lambda s: (0, 0), pipeline_mode=pl.Buffered(1)),
            pl.BlockSpec((tm, d), lambda s: (jnp.maximum(s - 1, 0) % per_seq, 0)),
            pl.BlockSpec((tm, d), lambda s: (jnp.maximum(s - 1, 0) % per_seq, 0)),
        ],
        out_specs=[row_spec(wc), row_spec(wr), row_spec(wr), row_spec(wr), row_spec(wr)],
        out_shape=[
            jax.ShapeDtypeStruct((T, wc), F32),
            jax.ShapeDtypeStruct((T, wr), BF16),
            jax.ShapeDtypeStruct((T, wr), F32),
            jax.ShapeDtypeStruct((T, wr), BF16),
            jax.ShapeDtypeStruct((T, wr), F32),
        ],
        scratch_shapes=[pltpu.VMEM((tm, D), BF16), pltpu.VMEM((tm, D), BF16)],
        compiler_params=pltpu.CompilerParams(
            dimension_semantics=("arbitrary",),
            vmem_limit_bytes=VMEM_LIMIT),
        name="in_proj",
    )(x2, norm_w.reshape(1, D), w_bf, cos_t, sin_t)


def _retention_kernel(lg_ref, q_ref, k_ref, v_ref, g_ref, nw_ref,
                      y_ref, kr_ref, kv_ref, r_ref, *, chunk):
    S, d = q_ref.shape
    C = chunk
    n_chunks = S // C
    h = pl.program_id(1)
    lg_f = lg_ref[0, h]
    lg_b = lg_ref[1, h]

    pos = lax.broadcasted_iota(jnp.int32, (C, d), 0).astype(F32)
    zeta_f = jnp.exp((C - 1 - pos) * lg_f)
    xi_f = jnp.exp((pos + 1.0) * lg_f)
    zeta_b = jnp.exp(pos * lg_b)
    xi_b = jnp.exp((C - pos) * lg_b)
    row = lax.broadcasted_iota(jnp.int32, (C, C), 0)
    col = lax.broadcasted_iota(jnp.int32, (C, C), 1)
    diff = (row - col).astype(F32)
    decay = jnp.exp(jnp.where(diff >= 0, diff * lg_f, -diff * lg_b))

    def prep(n, carry):
        r = pl.multiple_of(n * C, C)
        k = k_ref[pl.ds(r, C), :]
        kr_ref[pl.ds(r, C), :] = k.astype(BF16)
        kz = jnp.concatenate([k * zeta_f, k * zeta_b], axis=1).astype(BF16)
        kv_ref[n] = lax.dot_general(kz, v_ref[pl.ds(r, C), :], (((0,), (0,)), ((), ())),
                                    preferred_element_type=F32)
        return carry
    lax.fori_loop(0, n_chunks, prep, 0, unroll=8)

    dec_f = jnp.exp(C * lg_f)
    dec_b = jnp.exp(C * lg_b)

    def scan_f(n, state):
        r_ref[n, :, 0:d] = state.astype(BF16)
        return dec_f * state + kv_ref[n, 0:d, :]
    lax.fori_loop(0, n_chunks, scan_f, jnp.zeros((d, d), F32))

    def scan_b(t, state):
        n = n_chunks - 1 - t
        r_ref[n, :, d:2 * d] = state.astype(BF16)
        return dec_b * state + kv_ref[n, d:2 * d, :]
    lax.fori_loop(0, n_chunks, scan_b, jnp.zeros((d, d), F32))

    nw = nw_ref[...]

    def out(n, carry):
        r = pl.multiple_of(n * C, C)
        q = q_ref[pl.ds(r, C), :]
        s = lax.dot_general(q, kr_ref[pl.ds(r, C), :], (((1,), (1,)), ((), ())),
                            preferred_element_type=F32)
        p = (s * decay).astype(BF16)
        o = _dot(p, v_ref[pl.ds(r, C), :])
        cross = _dot(q, r_ref[n])
        o = o + cross[:, 0:d] * xi_f + cross[:, d:2 * d] * xi_b
        o = _rms_rows(o, nw)
        g = g_ref[pl.ds(r, C), :]
        y_ref[pl.ds(r, C), :] = (g * _sigmoid(g) * o).astype(y_ref.dtype)
        return carry
    lax.fori_loop(0, n_chunks, out, 0, unroll=8)


def _retention(q3, k3, v3, g3, lg, norm_w):
    B, S, width = q3.shape
    H = RET_HEADS
    d = width // H
    C = min(RET_CHUNK, S)

    def head_spec():
        return pl.BlockSpec((None, S, d), lambda b, h: (b, 0, h))

    return pl.pallas_call(
        functools.partial(_retention_kernel, chunk=C),
        grid=(B, H),
        in_specs=[
            pl.BlockSpec(memory_space=pltpu.SMEM),
            head_spec(), head_spec(), head_spec(), head_spec(),
            pl.BlockSpec((1, d), lambda b, h: (0, h)),
        ],
        out_specs=head_spec(),
        out_shape=jax.ShapeDtypeStruct((B, S, width), BF16),
        scratch_shapes=[
            pltpu.VMEM((S, d), BF16),
            pltpu.VMEM((S // C, 2 * d, d), F32),
            pltpu.VMEM((S // C, d, 2 * d), BF16),
        ],
        compiler_params=pltpu.CompilerParams(
            dimension_semantics=("parallel", "arbitrary"),
            vmem_limit_bytes=VMEM_LIMIT),
        name="retention",
    )(lg, q3, k3, v3, g3, norm_w.reshape(1, width))


def _conv_kernel(x_ref, xp_ref, xn_ref, cw_ref, cb_ref, lw_ref, lb_ref,
                 y_ref, xs_ref, cv_ref, *, row_block, norm_block):
    tm, W = x_ref.shape
    n_lt = W // LANES
    i = pl.program_id(1)
    last = pl.num_programs(1) - 1
    off0 = HALO - CONV_PAD

    for t in range(n_lt):
        ls = slice(t * LANES, (t + 1) * LANES)
        xs_ref[t, HALO:HALO + tm, :] = x_ref[:, ls]
        xs_ref[t, 0:HALO, :] = jnp.where(i > 0, xp_ref[:, ls], 0.0)
        xs_ref[t, HALO + tm:2 * HALO + tm, :] = jnp.where(i < last, xn_ref[:, ls], 0.0)

    reps = row_block // SUBLANES

    def conv_tile(idx, carry):
        rb = idx // n_lt
        t = idx % n_lt
        r0 = rb * row_block
        acc = jnp.concatenate([cb_ref[t]] * reps, axis=0)
        for k in range(CONV_KERNEL):
            wk = jnp.concatenate([cw_ref[t, k]] * reps, axis=0)
            acc = acc + xs_ref[t, pl.ds(r0 + (off0 + k), row_block), :] * wk
        cv_ref[t, pl.ds(pl.multiple_of(r0, row_block), row_block), :] = acc
        return carry
    lax.fori_loop(0, (tm // row_block) * n_lt, conv_tile, 0)

    def norm_rows(rb, carry):
        r0 = pl.multiple_of(rb * norm_block, norm_block)
        c = jnp.concatenate([cv_ref[t, pl.ds(r0, norm_block), :] for t in range(n_lt)], axis=1)
        mu = jnp.mean(c, axis=-1, keepdims=True)
        cc = c - mu
        var = jnp.mean(cc * cc, axis=-1, keepdims=True)
        z = cc * lax.rsqrt(var + EPS) * lw_ref[...] + lb_ref[...]
        y_ref[pl.ds(r0, norm_block), :] = (z * _sigmoid(z)).astype(y_ref.dtype)
        return carry
    lax.fori_loop(0, tm // norm_block, norm_rows, 0, unroll=4)


def _conv_mixer(x3, conv_w, conv_b, ln_w, ln_b, *, tm=256):
    B, S, W = x3.shape
    n_lt = W // LANES
    tm = min(tm, S)
    hb = tm // HALO
    n_halo = S // HALO
    cw8 = jnp.broadcast_to(conv_w.reshape(CONV_KERNEL, n_lt, 1, LANES).transpose(1, 0, 2, 3),
                           (n_lt, CONV_KERNEL, SUBLANES, LANES))
    cb8 = jnp.broadcast_to(conv_b.reshape(n_lt, 1, LANES), (n_lt, SUBLANES, LANES))
    return pl.pallas_call(
        functools.partial(_conv_kernel, row_block=min(128, tm), norm_block=32),
        grid=(B, S // tm),
        in_specs=[
            pl.BlockSpec((None, tm, W), lambda b, i: (b, i, 0)),
            pl.BlockSpec((None, HALO, W), lambda b, i: (b, jnp.maximum(i * hb - 1, 0), 0)),
            pl.BlockSpec((None, HALO, W), lambda b, i: (b, jnp.minimum((i + 1) * hb, n_halo - 1), 0)),
            pl.BlockSpec((n_lt, CONV_KERNEL, SUBLANES, LANES), lambda b, i: (0, 0, 0, 0)),
            pl.BlockSpec((n_lt, SUBLANES, LANES), lambda b, i: (0, 0, 0)),
            pl.BlockSpec((1, W), lambda b, i: (0, 0)),
            pl.BlockSpec((1, W), lambda b, i: (0, 0)),
        ],
        out_specs=pl.BlockSpec((None, tm, W), lambda b, i: (b, i, 0)),
        out_shape=jax.ShapeDtypeStruct((B, S, W), BF16),
        scratch_shapes=[pltpu.VMEM((n_lt, tm + 2 * HALO, LANES), F32),
                        pltpu.VMEM((n_lt, tm, LANES), F32)],
        compiler_params=pltpu.CompilerParams(
            dimension_semantics=("parallel", "arbitrary"),
            vmem_limit_bytes=VMEM_LIMIT),
        name="conv_mixer",
    )(x3, x3, x3, cw8, cb8, ln_w.reshape(1, W), ln_b.reshape(1, W))


def _out_proj_kernel(x_ref, yc_ref, yr_ref, w_ref, n2_ref, o_ref, h2_ref, *, row_chunk):
    tm = x_ref.shape[0]
    wc = yc_ref.shape[1]
    half = tm // 2
    for part in range(2):
        rows = slice(part * half, (part + 1) * half)
        acc = _dot(yc_ref[rows, :], w_ref[0:wc, :]) + _dot(yr_ref[rows, :], w_ref[wc:, :])
        for c in range(half // row_chunk):
            sub = slice(part * half + c * row_chunk, part * half + (c + 1) * row_chunk)
            x1 = x_ref[sub, :] + acc[c * row_chunk:(c + 1) * row_chunk, :]
            o_ref[sub, :] = x1
            h2_ref[sub, :] = _rms_rows(x1, n2_ref[...]).astype(h2_ref.dtype)


def _out_proj(x2, yc2, yr2, w_bf, norm2_w, *, tm=512):
    T, D = x2.shape
    wc, wr = yc2.shape[1], yr2.shape[1]
    tm = min(tm, T)
    return pl.pallas_call(
        functools.partial(_out_proj_kernel, row_chunk=64),
        grid=(T // tm,),
        in_specs=[
            pl.BlockSpec((tm, D), lambda i: (i, 0)),
            pl.BlockSpec((tm, wc), lambda i: (i, 0)),
            pl.BlockSpec((tm, wr), lambda i: (i, 0)),
            pl.BlockSpec((wc + wr, D), lambda i: (0, 0)),
            pl.BlockSpec((1, D), lambda i: (0, 0)),
        ],
        out_specs=[pl.BlockSpec((tm, D), lambda i: (i, 0)),
                   pl.BlockSpec((tm, D), lambda i: (i, 0))],
        out_shape=[jax.ShapeDtypeStruct((T, D), F32), jax.ShapeDtypeStruct((T, D), BF16)],
        compiler_params=pltpu.CompilerParams(
            dimension_semantics=("parallel",),
            vmem_limit_bytes=VMEM_LIMIT),
        name="out_proj",
    )(x2, yc2, yr2, w_bf, norm2_w.reshape(1, D))


def _mix_out_kernel(x_ref, hg_ref, hp_ref, hn_ref, yr_ref, w_ref, cw_ref, cb_ref, lw_ref, lb_ref,
                    n2_ref, o_ref, h2_ref, xs_ref, cv_ref, *, parts, norm_block, row_chunk):
    tm, W = hg_ref.shape
    n_lt = W // LANES
    i = pl.program_id(1)
    last = pl.num_programs(1) - 1
    off0 = HALO - CONV_PAD
    part_rows = tm // parts
    reps = part_rows // SUBLANES

    for t in range(n_lt):
        ls = slice(t * LANES, (t + 1) * LANES)
        xs_ref[t, HALO:HALO + tm, :] = hg_ref[:, ls]
        xs_ref[t, 0:HALO, :] = jnp.where(i > 0, hp_ref[:, ls], 0.0)
        xs_ref[t, HALO + tm:2 * HALO + tm, :] = jnp.where(i < last, hn_ref[:, ls], 0.0)

    for p in range(parts):
        r0 = p * part_rows
        for t in range(n_lt):
            acc = jnp.concatenate([cb_ref[t]] * reps, axis=0)
            for k in range(CONV_KERNEL):
                wk = jnp.concatenate([cw_ref[t, k]] * reps, axis=0)
                acc = acc + xs_ref[t, r0 + off0 + k:r0 + off0 + k + part_rows, :] * wk
            cv_ref[t, r0:r0 + part_rows, :] = acc
        ycs = []
        for nb in range(part_rows // norm_block):
            rr = slice(r0 + nb * norm_block, r0 + (nb + 1) * norm_block)
            c = jnp.concatenate([cv_ref[t, rr, :] for t in range(n_lt)], axis=1)
            mu = jnp.mean(c, axis=-1, keepdims=True)
            cc = c - mu
            var = jnp.mean(cc * cc, axis=-1, keepdims=True)
            z = cc * lax.rsqrt(var + EPS) * lw_ref[...] + lb_ref[...]
            ycs.append((z * _sigmoid(z)).astype(BF16))
        yc = jnp.concatenate(ycs, axis=0)
        acc = _dot(yc, w_ref[0:W, :]) + _dot(yr_ref[r0:r0 + part_rows, :], w_ref[W:, :])
        for c in range(part_rows // row_chunk):
            sub = slice(r0 + c * row_chunk, r0 + (c + 1) * row_chunk)
            x1 = x_ref[sub, :] + acc[c * row_chunk:(c + 1) * row_chunk, :]
            o_ref[sub, :] = x1
            h2_ref[sub, :] = _rms_rows(x1, n2_ref[...]).astype(h2_ref.dtype)


def _mix_out(x3, hg3, yr3, w_bf, conv_w, conv_b, ln_w, ln_b, norm2_w, *, tm=512):
    B, S, D = x3.shape
    W = hg3.shape[-1]
    wr = yr3.shape[-1]
    n_lt = W // LANES
    tm = min(tm, S)
    hb = tm // HALO
    n_halo = S // HALO
    cw8 = jnp.broadcast_to(conv_w.reshape(CONV_KERNEL, n_lt, 1, LANES).transpose(1, 0, 2, 3),
                           (n_lt, CONV_KERNEL, SUBLANES, LANES))
    cb8 = jnp.broadcast_to(conv_b.reshape(n_lt, 1, LANES), (n_lt, SUBLANES, LANES))

    def rows(width):
        return pl.BlockSpec((None, tm, width), lambda b, i: (b, i, 0))

    def const(shape):
        return pl.BlockSpec(shape, lambda b, i: (0,) * len(shape))

    return pl.pallas_call(
        functools.partial(_mix_out_kernel, parts=tm // 128, norm_block=32, row_chunk=64),
        grid=(B, S // tm),
        in_specs=[
            rows(D),
            rows(W),
            pl.BlockSpec((None, HALO, W), lambda b, i: (b, jnp.maximum(i * hb - 1, 0), 0)),
            pl.BlockSpec((None, HALO, W), lambda b, i: (b, jnp.minimum((i + 1) * hb, n_halo - 1), 0)),
            rows(wr),
            const((W + wr, D)),
            const((n_lt, CONV_KERNEL, SUBLANES, LANES)),
            const((n_lt, SUBLANES, LANES)),
            const((1, W)), const((1, W)), const((1, D)),
        ],
        out_specs=[rows(D), rows(D)],
        out_shape=[jax.ShapeDtypeStruct((B, S, D), F32), jax.ShapeDtypeStruct((B, S, D), BF16)],
        scratch_shapes=[pltpu.VMEM((n_lt, tm + 2 * HALO, LANES), F32),
                        pltpu.VMEM((n_lt, tm, LANES), F32)],
        compiler_params=pltpu.CompilerParams(
            dimension_semantics=("parallel", "arbitrary"),
            vmem_limit_bytes=VMEM_LIMIT),
        name="mix_out",
    )(x3, hg3, hg3, hg3, yr3, w_bf, cw8, cb8, ln_w.reshape(1, W), ln_b.reshape(1, W),
      norm2_w.reshape(1, D))


def _ffn_kernel(h_ref, x_hbm, wg_ref, wu_ref, wo_ref, fn_ref, o_ref, sem,
                *, row_chunk, final_norm):
    tm = h_ref.shape[0]
    tf = wg_ref.shape[1]
    half = tf // 2
    i = pl.program_id(0)
    j = pl.program_id(1)

    def residual_copy():
        return pltpu.make_async_copy(x_hbm.at[pl.ds(i * tm, tm), :], o_ref, sem)

    def hidden():
        h = h_ref[...]
        g0 = _dot(h, wg_ref[:, 0:half])
        u0 = _dot(h, wu_ref[:, 0:half])
        g1 = _dot(h, wg_ref[:, half:tf])
        u1 = _dot(h, wu_ref[:, half:tf])
        a0 = (g0 * _sigmoid(g0) * u0).astype(BF16)
        a1 = (g1 * _sigmoid(g1) * u1).astype(BF16)
        return a0, a1

    def accumulate(a0, a1):
        o_ref[...] += _dot(a0, wo_ref[0:half, :]) + _dot(a1, wo_ref[half:tf, :])

    @pl.when(j == 0)
    def _():
        residual_copy().start()
        a0, a1 = hidden()
        residual_copy().wait()
        accumulate(a0, a1)

    @pl.when(j > 0)
    def _():
        accumulate(*hidden())

    if final_norm:
        @pl.when(j == pl.num_programs(1) - 1)
        def _():
            for c in range(tm // row_chunk):
                rows = slice(c * row_chunk, (c + 1) * row_chunk)
                o_ref[rows, :] = _rms_rows(o_ref[rows, :], fn_ref[...])


def _ffn(h2, x1, w_in_bf, w_out_bf, final_w, *, final_norm, tm=1024, tf=512):
    T, D = x1.shape
    F = w_out_bf.shape[0]
    tm = min(tm, T)
    nf = F // tf
    assert F % tf == 0 and T % tm == 0
    return pl.pallas_call(
        functools.partial(_ffn_kernel, row_chunk=64, final_norm=final_norm),
        grid=(T // tm, nf),
        in_specs=[
            pl.BlockSpec((tm, D), lambda i, j: (i, 0)),
            pl.BlockSpec(memory_space=pl.ANY),
            pl.BlockSpec((D, tf), lambda i, j: (0, j)),
            pl.BlockSpec((D, tf), lambda i, j: (0, j + nf)),
            pl.BlockSpec((tf, D), lambda i, j: (j, 0)),
            pl.BlockSpec((1, D), lambda i, j: (0, 0)),
        ],
        out_specs=pl.BlockSpec((tm, D), lambda i, j: (i, 0)),
        out_shape=jax.ShapeDtypeStruct((T, D), F32),
        scratch_shapes=[pltpu.SemaphoreType.DMA],
        compiler_params=pltpu.CompilerParams(
            dimension_semantics=("parallel", "arbitrary"),
            vmem_limit_bytes=VMEM_LIMIT),
        name="ffn",
    )(h2, x1, w_in_bf, w_in_bf, w_out_bf, final_w.reshape(1, D))


def _rope_tables(S, d):
    inv_freq = ROPE_BASE ** (-jnp.arange(0, d, 2, dtype=F32) / d)
    ang = jnp.arange(S, dtype=F32)[:, None] * inv_freq[None, :]
    cos, sin = jnp.cos(ang), jnp.sin(ang)
    return jnp.concatenate([cos, cos], axis=1), jnp.concatenate([-sin, sin], axis=1)


def _trunk(x, p):
    B, S, D = x.shape
    depth = p["w_in_bf"].shape[0]
    wc = p["conv_w"].shape[-1]
    wr = p["ret_norm_w"].shape[-1]
    cos_t, sin_t = _rope_tables(S, wr // RET_HEADS)
    x2 = x.reshape(B * S, D)
    for l in range(depth):
        last = l == depth - 1
        hg, q, k, v, g = _in_proj(x2, p["norm1_w"][l], p["w_in_bf"][l], cos_t, sin_t,
                                  seq=S, wc=wc, wr=wr)
        lg = jnp.stack([p["ret_log_decay_fwd"][l], p["ret_log_decay_bwd"][l]]).astype(F32)
        yr = _retention(q.reshape(B, S, wr), k.reshape(B, S, wr), v.reshape(B, S, wr),
                        g.reshape(B, S, wr), lg, p["ret_norm_w"][l])
        x1, h2 = _mix_out(x2.reshape(B, S, D), hg.reshape(B, S, wc), yr, p["w_out_bf"][l],
                          p["conv_w"][l], p["conv_b"][l], p["conv_ln_w"][l], p["conv_ln_b"][l],
                          p["norm2_w"][l])
        x2 = _ffn(h2.reshape(B * S, D), x1.reshape(B * S, D), p["w_ffn_in_bf"][l],
                  p["w_ffn_out_bf"][l], p["final_norm_w"], final_norm=last)
    return x2.reshape(B, S, D)


def kernel(x_prompt, x_sample, norm1_w, w_in, conv_w, conv_b, conv_ln_w, conv_ln_b,
           ret_log_decay_fwd, ret_log_decay_bwd, ret_norm_w, w_out,
           norm2_w, w_ffn_in, w_ffn_out, final_norm_w):
    p = dict(norm1_w=norm1_w, conv_w=conv_w, conv_b=conv_b, conv_ln_w=conv_ln_w,
             conv_ln_b=conv_ln_b, ret_log_decay_fwd=ret_log_decay_fwd,
             ret_log_decay_bwd=ret_log_decay_bwd, ret_norm_w=ret_norm_w,
             norm2_w=norm2_w, final_norm_w=final_norm_w,
             w_in_bf=w_in.astype(BF16), w_out_bf=w_out.astype(BF16),
             w_ffn_in_bf=w_ffn_in.astype(BF16), w_ffn_out_bf=w_ffn_out.astype(BF16))
    return _trunk(x_prompt, p), _trunk(x_sample, p)
```

```python
import functools

import jax
import jax.numpy as jnp
from jax import lax
from jax.experimental import pallas as pl
from jax.experimental.pallas import tpu as pltpu

F32 = jnp.float32
BF16 = jnp.bfloat16

EPS = 1e-6
ROPE_BASE = 10000.0
RET_HEADS = 8
CONV_KERNEL = 31
CONV_PAD = CONV_KERNEL // 2
SUBLANES = 8
LANES = 128
HALO = 16
RET_CHUNK = 256
VMEM_LIMIT = 56 * 1024 * 1024


def _sigmoid(x):
    return 1.0 / (1.0 + jnp.exp(-x))


def _rms_rows(x, w):
    ms = jnp.mean(x * x, axis=-1, keepdims=True)
    return x * lax.rsqrt(ms + EPS) * w


def _dot(a, b):
    return jnp.dot(a, b, preferred_element_type=F32)


def _in_proj_kernel(x_ref, nw_ref, w_ref, cos_ref, sin_ref,
                    hg_ref, q_ref, k_ref, v_ref, g_ref, h0_ref, h1_ref,
                    *, wc, wr, heads, row_chunk):
    s = pl.program_id(0)
    tm = x_ref.shape[0]
    d = wr // heads

    @pl.when(s == 0)
    def _():
        h1_ref[...] = jnp.zeros(h1_ref.shape, h1_ref.dtype)

    def step(h_read, h_write):
        h = h_read[...]

        def proj(c0, width):
            return _dot(h, w_ref[:, c0:c0 + width])

        hg_ref[...] = proj(0, wc) * _sigmoid(proj(wc, wc))
        cos = cos_ref[...]
        sin = sin_ref[...]

        def rotary(x, hd):
            xh = x[:, hd * d:(hd + 1) * d]
            return xh * cos + pltpu.roll(xh, d // 2, axis=1) * sin

        q = proj(2 * wc, wr)
        for hd in range(heads):
            q_ref[:, hd * d:(hd + 1) * d] = rotary(q, hd).astype(q_ref.dtype)
        k = proj(2 * wc + wr, wr)
        for hd in range(heads):
            k_ref[:, hd * d:(hd + 1) * d] = rotary(k, hd) * (d ** -0.5)
        v_ref[...] = proj(2 * wc + 2 * wr, wr).astype(v_ref.dtype)
        g_ref[...] = proj(2 * wc + 3 * wr, wr)
        for c in range(tm // row_chunk):
            rows = slice(c * row_chunk, (c + 1) * row_chunk)
            h_write[rows, :] = _rms_rows(x_ref[rows, :], nw_ref[...]).astype(BF16)

    @pl.when(s % 2 == 0)
    def _():
        step(h1_ref, h0_ref)

    @pl.when(s % 2 == 1)
    def _():
        step(h0_ref, h1_ref)


def _in_proj(x2, norm_w, w_bf, cos_t, sin_t, *, seq, wc, wr, tm=256):
    T, D = x2.shape
    N = w_bf.shape[1]
    tm = min(tm, seq)
    n = T // tm
    d = wr // RET_HEADS
    per_seq = seq // tm

    def row_spec(width):
        return pl.BlockSpec((tm, width), lambda s: (jnp.maximum(s - 1, 0), 0))

    return pl.pallas_call(
        functools.partial(_in_proj_kernel, wc=wc, wr=wr, heads=RET_HEADS, row_chunk=64),
        grid=(n + 1,),
        in_specs=[
            pl.BlockSpec((tm, D), lambda s: (jnp.minimum(s, n - 1), 0)),
            pl.BlockSpec((1, D), lambda s: (0, 0)),
            pl.BlockSpec((D, N), lambda s: (0, 0), pipeline_mode=pl.Buffered(1)),
            pl.BlockSpec((tm, d), lambda s: (jnp.maximum(s - 1, 0) % per_seq, 0)),
            pl.BlockSpec((tm, d), lambda s: (jnp.maximum(s - 1, 0) % per_seq, 0)),
        ],
        out_specs=[row_spec(wc), row_spec(wr), row_spec(wr), row_spec(wr), row_spec(wr)],
        out_shape=[
            jax.ShapeDtypeStruct((T, wc), F32),
            jax.ShapeDtypeStruct((T, wr), BF16),
            jax.ShapeDtypeStruct((T, wr), F32),
            jax.ShapeDtypeStruct((T, wr), BF16),
            jax.ShapeDtypeStruct((T, wr), F32),
        ],
        scratch_shapes=[pltpu.VMEM((tm, D), BF16), pltpu.VMEM((tm, D), BF16)],
        compiler_params=pltpu.CompilerParams(
            dimension_semantics=("arbitrary",),
            vmem_limit_bytes=VMEM_LIMIT),
        name="in_proj",
    )(x2, norm_w.reshape(1, D), w_bf, cos_t, sin_t)


def _retention_kernel(lg_ref, q_ref, k_ref, v_ref, g_ref, nw_ref,
                      y_ref, kr_ref, kv_ref, r_ref, *, chunk, d):
    for hh in range(q_ref.shape[1] // d):
        head = pl.program_id(1) * (q_ref.shape[1] // d) + hh
        lanes = slice(hh * d, (hh + 1) * d)
        _retention_head(lg_ref[0, head], lg_ref[1, head], q_ref, k_ref, v_ref, g_ref, nw_ref,
                        y_ref, kr_ref, kv_ref, r_ref, lanes=lanes, chunk=chunk)


def _retention_head(lg_f, lg_b, q_ref, k_ref, v_ref, g_ref, nw_ref,
                    y_ref, kr_ref, kv_ref, r_ref, *, lanes, chunk):
    S = q_ref.shape[0]
    d = lanes.stop - lanes.start
    C = chunk
    n_chunks = S // C

    pos = lax.broadcasted_iota(jnp.int32, (C, d), 0).astype(F32)
    zeta_f = jnp.exp((C - 1 - pos) * lg_f)
    xi_f = jnp.exp((pos + 1.0) * lg_f)
    zeta_b = jnp.exp(pos * lg_b)
    xi_b = jnp.exp((C - pos) * lg_b)
    row = lax.broadcasted_iota(jnp.int32, (C, C), 0)
    col = lax.broadcasted_iota(jnp.int32, (C, C), 1)
    diff = (row - col).astype(F32)
    decay = jnp.exp(jnp.where(diff >= 0, diff * lg_f, -diff * lg_b))

    def prep(n, carry):
        r = pl.multiple_of(n * C, C)
        k = k_ref[pl.ds(r, C), lanes]
        kr_ref[pl.ds(r, C), :] = k.astype(BF16)
        kz = jnp.concatenate([k * zeta_f, k * zeta_b], axis=1).astype(BF16)
        kv_ref[n] = lax.dot_general(kz, v_ref[pl.ds(r, C), lanes], (((0,), (0,)), ((), ())),
                                    preferred_element_type=F32)
        return carry
    lax.fori_loop(0, n_chunks, prep, 0, unroll=8)

    dec_f = jnp.exp(C * lg_f)
    dec_b = jnp.exp(C * lg_b)

    def scan_f(n, state):
        r_ref[n, :, 0:d] = state.astype(BF16)
        return dec_f * state + kv_ref[n, 0:d, :]
    lax.fori_loop(0, n_chunks, scan_f, jnp.zeros((d, d), F32))

    def scan_b(t, state):
        n = n_chunks - 1 - t
        r_ref[n, :, d:2 * d] = state.astype(BF16)
        return dec_b * state + kv_ref[n, d:2 * d, :]
    lax.fori_loop(0, n_chunks, scan_b, jnp.zeros((d, d), F32))

    nw = nw_ref[:, lanes]

    def out(n, carry):
        r = pl.multiple_of(n * C, C)
        q = q_ref[pl.ds(r, C), lanes]
        s = lax.dot_general(q, kr_ref[pl.ds(r, C), :], (((1,), (1,)), ((), ())),
                            preferred_element_type=F32)
        p = (s * decay).astype(BF16)
        o = _dot(p, v_ref[pl.ds(r, C), lanes])
        cross = _dot(q, r_ref[n])
        o = o + cross[:, 0:d] * xi_f + cross[:, d:2 * d] * xi_b
        o = _rms_rows(o, nw)
        g = g_ref[pl.ds(r, C), lanes]
        y_ref[pl.ds(r, C), lanes] = (g * _sigmoid(g) * o).astype(y_ref.dtype)
        return carry
    lax.fori_loop(0, n_chunks, out, 0, unroll=8)


def _retention(q3, k3, v3, g3, lg, norm_w):
    B, S, width = q3.shape
    H = RET_HEADS
    d = width // H
    C = min(RET_CHUNK, S)
    hps = 2 if H % 2 == 0 else 1

    def head_spec():
        return pl.BlockSpec((None, S, hps * d), lambda b, h: (b, 0, h))

    return pl.pallas_call(
        functools.partial(_retention_kernel, chunk=C, d=d),
        grid=(B, H // hps),
        in_specs=[
            pl.BlockSpec(memory_space=pltpu.SMEM),
            head_spec(), head_spec(), head_spec(), head_spec(),
            pl.BlockSpec((1, hps * d), lambda b, h: (0, h)),
        ],
        out_specs=head_spec(),
        out_shape=jax.ShapeDtypeStruct((B, S, width), BF16),
        scratch_shapes=[
            pltpu.VMEM((S, d), BF16),
            pltpu.VMEM((S // C, 2 * d, d), F32),
            pltpu.VMEM((S // C, d, 2 * d), BF16),
        ],
        compiler_params=pltpu.CompilerParams(
            dimension_semantics=("parallel", "arbitrary"),
            vmem_limit_bytes=VMEM_LIMIT),
        name="retention",
    )(lg, q3, k3, v3, g3, norm_w.reshape(1, width))


def _mix_out_kernel(x_ref, hg_ref, hp_ref, hn_ref, yr_ref, w_ref, cw_ref, cb_ref, lw_ref, lb_ref,
                    n2_ref, o_ref, h2_ref, xs_ref, cv_ref,
                    *, parts, conv_rows, norm_block, row_chunk):
    tm, W = hg_ref.shape
    n_lt = W // LANES
    i = pl.program_id(1)
    last = pl.num_programs(1) - 1
    off0 = HALO - CONV_PAD
    part_rows = tm // parts
    reps = conv_rows // SUBLANES

    for t in range(n_lt):
        ls = slice(t * LANES, (t + 1) * LANES)
        xs_ref[t, HALO:HALO + tm, :] = hg_ref[:, ls]
        xs_ref[t, 0:HALO, :] = jnp.where(i > 0, hp_ref[:, ls], 0.0)
        xs_ref[t, HALO + tm:2 * HALO + tm, :] = jnp.where(i < last, hn_ref[:, ls], 0.0)

    for p in range(parts):
        r0 = p * part_rows
        for rc in range(r0, r0 + part_rows, conv_rows):
            for t in range(n_lt):
                acc = jnp.concatenate([cb_ref[t]] * reps, axis=0)
                for k in range(CONV_KERNEL):
                    wk = jnp.concatenate([cw_ref[t, k]] * reps, axis=0)
                    acc = acc + xs_ref[t, rc + off0 + k:rc + off0 + k + conv_rows, :] * wk
                cv_ref[t, rc:rc + conv_rows, :] = acc
        ycs = []
        for nb in range(part_rows // norm_block):
            rr = slice(r0 + nb * norm_block, r0 + (nb + 1) * norm_block)
            c = jnp.concatenate([cv_ref[t, rr, :] for t in range(n_lt)], axis=1)
            mu = jnp.mean(c, axis=-1, keepdims=True)
            cc = c - mu
            var = jnp.mean(cc * cc, axis=-1, keepdims=True)
            z = cc * lax.rsqrt(var + EPS) * lw_ref[...] + lb_ref[...]
            ycs.append((z * _sigmoid(z)).astype(BF16))
        yc = jnp.concatenate(ycs, axis=0)
        acc = _dot(yc, w_ref[0:W, :]) + _dot(yr_ref[r0:r0 + part_rows, :], w_ref[W:, :])
        for c in range(part_rows // row_chunk):
            sub = slice(r0 + c * row_chunk, r0 + (c + 1) * row_chunk)
            x1 = x_ref[sub, :] + acc[c * row_chunk:(c + 1) * row_chunk, :]
            o_ref[sub, :] = x1
            h2_ref[sub, :] = _rms_rows(x1, n2_ref[...]).astype(h2_ref.dtype)


def _mix_out(x3, hg3, yr3, w_bf, conv_w, conv_b, ln_w, ln_b, norm2_w, *, tm=512):
    B, S, D = x3.shape
    W = hg3.shape[-1]
    wr = yr3.shape[-1]
    n_lt = W // LANES
    tm = min(tm, S)
    hb = tm // HALO
    n_halo = S // HALO
    cw8 = jnp.broadcast_to(conv_w.reshape(CONV_KERNEL, n_lt, 1, LANES).transpose(1, 0, 2, 3),
                           (n_lt, CONV_KERNEL, SUBLANES, LANES))
    cb8 = jnp.broadcast_to(conv_b.reshape(n_lt, 1, LANES), (n_lt, SUBLANES, LANES))

    def rows(width):
        return pl.BlockSpec((None, tm, width), lambda b, i: (b, i, 0))

    def const(shape):
        return pl.BlockSpec(shape, lambda b, i: (0,) * len(shape))

    return pl.pallas_call(
        functools.partial(_mix_out_kernel, parts=max(tm // 256, 1), conv_rows=min(128, tm),
                          norm_block=32, row_chunk=64),
        grid=(B, S // tm),
        in_specs=[
            rows(D),
            rows(W),
            pl.BlockSpec((None, HALO, W), lambda b, i: (b, jnp.maximum(i * hb - 1, 0), 0)),
            pl.BlockSpec((None, HALO, W), lambda b, i: (b, jnp.minimum((i + 1) * hb, n_halo - 1), 0)),
            rows(wr),
            const((W + wr, D)),
            const((n_lt, CONV_KERNEL, SUBLANES, LANES)),
            const((n_lt, SUBLANES, LANES)),
            const((1, W)), const((1, W)), const((1, D)),
        ],
        out_specs=[rows(D), rows(D)],
        out_shape=[jax.ShapeDtypeStruct((B, S, D), F32), jax.ShapeDtypeStruct((B, S, D), BF16)],
        scratch_shapes=[pltpu.VMEM((n_lt, tm + 2 * HALO, LANES), F32),
                        pltpu.VMEM((n_lt, tm, LANES), F32)],
        compiler_params=pltpu.CompilerParams(
            dimension_semantics=("parallel", "arbitrary"),
            vmem_limit_bytes=VMEM_LIMIT),
        name="mix_out",
    )(x3, hg3, hg3, hg3, yr3, w_bf, cw8, cb8, ln_w.reshape(1, W), ln_b.reshape(1, W),
      norm2_w.reshape(1, D))


def _ffn_kernel(h_ref, x_hbm, wg_ref, wu_ref, wo_ref, fn_ref, o_ref, sem,
                *, row_chunk, final_norm):
    tm = h_ref.shape[0]
    tf = wg_ref.shape[1]
    half = tf // 2
    i = pl.program_id(0)
    j = pl.program_id(1)

    def residual_copy():
        return pltpu.make_async_copy(x_hbm.at[pl.ds(i * tm, tm), :], o_ref, sem)

    def hidden():
        h = h_ref[...]
        g0 = _dot(h, wg_ref[:, 0:half])
        u0 = _dot(h, wu_ref[:, 0:half])
        g1 = _dot(h, wg_ref[:, half:tf])
        u1 = _dot(h, wu_ref[:, half:tf])
        a0 = (g0 * _sigmoid(g0) * u0).astype(BF16)
        a1 = (g1 * _sigmoid(g1) * u1).astype(BF16)
        return a0, a1

    def accumulate(a0, a1):
        o_ref[...] += _dot(a0, wo_ref[0:half, :]) + _dot(a1, wo_ref[half:tf, :])

    @pl.when(j == 0)
    def _():
        residual_copy().start()
        a0, a1 = hidden()
        residual_copy().wait()
        accumulate(a0, a1)

    @pl.when(j > 0)
    def _():
        accumulate(*hidden())

    if final_norm:
        @pl.when(j == pl.num_programs(1) - 1)
        def _():
            for c in range(tm // row_chunk):
                rows = slice(c * row_chunk, (c + 1) * row_chunk)
                o_ref[rows, :] = _rms_rows(o_ref[rows, :], fn_ref[...])


def _ffn(h2, x1, w_in_bf, w_out_bf, final_w, *, final_norm, tm=1024, tf=512):
    T, D = x1.shape
    F = w_out_bf.shape[0]
    tm = min(tm, T)
    nf = F // tf
    assert F % tf == 0 and T % tm == 0
    return pl.pallas_call(
        functools.partial(_ffn_kernel, row_chunk=64, final_norm=final_norm),
        grid=(T // tm, nf),
        in_specs=[
            pl.BlockSpec((tm, D), lambda i, j: (i, 0)),
            pl.BlockSpec(memory_space=pl.ANY),
            pl.BlockSpec((D, tf), lambda i, j: (0, j)),
            pl.BlockSpec((D, tf), lambda i, j: (0, j + nf)),
            pl.BlockSpec((tf, D), lambda i, j: (j, 0)),
            pl.BlockSpec((1, D), lambda i, j: (0, 0)),
        ],
        out_specs=pl.BlockSpec((tm, D), lambda i, j: (i, 0)),
        out_shape=jax.ShapeDtypeStruct((T, D), F32),
        scratch_shapes=[pltpu.SemaphoreType.DMA],
        compiler_params=pltpu.CompilerParams(
            dimension_semantics=("parallel", "arbitrary"),
            vmem_limit_bytes=VMEM_LIMIT),
        name="ffn",
    )(h2, x1, w_in_bf, w_in_bf, w_out_bf, final_w.reshape(1, D))


def _rope_tables(S, d):
    inv_freq = ROPE_BASE ** (-jnp.arange(0, d, 2, dtype=F32) / d)
    ang = jnp.arange(S, dtype=F32)[:, None] * inv_freq[None, :]
    cos, sin = jnp.cos(ang), jnp.sin(ang)
    return jnp.concatenate([cos, cos], axis=1), jnp.concatenate([-sin, sin], axis=1)


def _trunk(x, p):
    B, S, D = x.shape
    depth = p["w_in_bf"].shape[0]
    wc = p["conv_w"].shape[-1]
    wr = p["ret_norm_w"].shape[-1]
    cos_t, sin_t = _rope_tables(S, wr // RET_HEADS)
    x2 = x.reshape(B * S, D)
    for l in range(depth):
        last = l == depth - 1
        hg, q, k, v, g = _in_proj(x2, p["norm1_w"][l], p["w_in_bf"][l], cos_t, sin_t,
                                  seq=S, wc=wc, wr=wr)
        lg = jnp.stack([p["ret_log_decay_fwd"][l], p["ret_log_decay_bwd"][l]]).astype(F32)
        yr = _retention(q.reshape(B, S, wr), k.reshape(B, S, wr), v.reshape(B, S, wr),
                        g.reshape(B, S, wr), lg, p["ret_norm_w"][l])
        x1, h2 = _mix_out(x2.reshape(B, S, D), hg.reshape(B, S, wc), yr, p["w_out_bf"][l],
                          p["conv_w"][l], p["conv_b"][l], p["conv_ln_w"][l], p["conv_ln_b"][l],
                          p["norm2_w"][l])
        x2 = _ffn(h2.reshape(B * S, D), x1.reshape(B * S, D), p["w_ffn_in_bf"][l],
                  p["w_ffn_out_bf"][l], p["final_norm_w"], final_norm=last)
    return x2.reshape(B, S, D)


def kernel(x_prompt, x_sample, norm1_w, w_in, conv_w, conv_b, conv_ln_w, conv_ln_b,
           ret_log_decay_fwd, ret_log_decay_bwd, ret_norm_w, w_out,
           norm2_w, w_ffn_in, w_ffn_out, final_norm_w):
    p = dict(norm1_w=norm1_w, conv_w=conv_w, conv_b=conv_b, conv_ln_w=conv_ln_w,
             conv_ln_b=conv_ln_b, ret_log_decay_fwd=ret_log_decay_fwd,
             ret_log_decay_bwd=ret_log_decay_bwd, ret_norm_w=ret_norm_w,
             norm2_w=norm2_w, final_norm_w=final_norm_w,
             w_in_bf=w_in.astype(BF16), w_out_bf=w_out.astype(BF16),
             w_ffn_in_bf=w_ffn_in.astype(BF16), w_ffn_out_bf=w_ffn_out.astype(BF16))
    return _trunk(x_prompt, p), _trunk(x_sample, p)
```

```python
import functools

import jax
import jax.numpy as jnp
from jax import lax
from jax.experimental import pallas as pl
from jax.experimental.pallas import tpu as pltpu

F32 = jnp.float32
BF16 = jnp.bfloat16

EPS = 1e-6
ROPE_BASE = 10000.0
RET_HEADS = 8
CONV_KERNEL = 31
CONV_PAD = CONV_KERNEL // 2
SUBLANES = 8
LANES = 128
HALO = 16
RET_CHUNK = 256
VMEM_LIMIT = 56 * 1024 * 1024


def _sigmoid(x):
    return 1.0 / (1.0 + jnp.exp(-x))


def _rms_rows(x, w):
    ms = jnp.mean(x * x, axis=-1, keepdims=True)
    return x * lax.rsqrt(ms + EPS) * w


def _dot(a, b):
    return jnp.dot(a, b, preferred_element_type=F32)


def _in_proj_kernel(x_ref, nw_ref, w_ref, cos_ref, sin_ref,
                    hg_ref, q_ref, k_ref, v_ref, g_ref, h0_ref, h1_ref,
                    *, wc, wr, heads, row_chunk):
    s = pl.program_id(0)
    tm = x_ref.shape[0]
    d = wr // heads

    def normalise(h_write):
        for c in range(tm // row_chunk):
            rows = slice(c * row_chunk, (c + 1) * row_chunk)
            h_write[rows, :] = _rms_rows(x_ref[rows, :], nw_ref[...]).astype(BF16)

    def step(h_read, h_write):
        h = h_read[...]

        def proj(c0, width):
            return _dot(h, w_ref[:, c0:c0 + width])

        hg_ref[...] = proj(0, wc) * _sigmoid(proj(wc, wc))
        cos = cos_ref[...]
        sin = sin_ref[...]

        def rotary(x, hd):
            xh = x[:, hd * d:(hd + 1) * d]
            return xh * cos + pltpu.roll(xh, d // 2, axis=1) * sin

        q = proj(2 * wc, wr)
        for hd in range(heads):
            q_ref[:, hd * d:(hd + 1) * d] = rotary(q, hd).astype(q_ref.dtype)
        k = proj(2 * wc + wr, wr)
        for hd in range(heads):
            k_ref[:, hd * d:(hd + 1) * d] = rotary(k, hd) * (d ** -0.5)
        v_ref[...] = proj(2 * wc + 2 * wr, wr).astype(v_ref.dtype)
        g_ref[...] = proj(2 * wc + 3 * wr, wr)
        normalise(h_write)

    @pl.when(s == 0)
    def _():
        normalise(h0_ref)

    @pl.when((s > 0) & (s % 2 == 0))
    def _():
        step(h1_ref, h0_ref)

    @pl.when(s % 2 == 1)
    def _():
        step(h0_ref, h1_ref)


def _in_proj(x2, norm_w, w_bf, cos_t, sin_t, *, seq, wc, wr, tm=256):
    T, D = x2.shape
    N = w_bf.shape[1]
    tm = min(tm, seq)
    n = T // tm
    d = wr // RET_HEADS
    per_seq = seq // tm

    def row_spec(width):
        return pl.BlockSpec((tm, width), lambda s: (jnp.maximum(s - 1, 0), 0))

    return pl.pallas_call(
        functools.partial(_in_proj_kernel, wc=wc, wr=wr, heads=RET_HEADS, row_chunk=64),
        grid=(n + 1,),
        in_specs=[
            pl.BlockSpec((tm, D), lambda s: (jnp.minimum(s, n - 1), 0)),
            pl.BlockSpec((1, D), lambda s: (0, 0)),
            pl.BlockSpec((D, N), lambda s: (0, 0), pipeline_mode=pl.Buffered(1)),
            pl.BlockSpec((tm, d), lambda s: (jnp.maximum(s - 1, 0) % per_seq, 0)),
            pl.BlockSpec((tm, d), lambda s: (jnp.maximum(s - 1, 0) % per_seq, 0)),
        ],
        out_specs=[row_spec(wc), row_spec(wr), row_spec(wr), row_spec(wr), row_spec(wr)],
        out_shape=[
            jax.ShapeDtypeStruct((T, wc), F32),
            jax.ShapeDtypeStruct((T, wr), BF16),
            jax.ShapeDtypeStruct((T, wr), F32),
            jax.ShapeDtypeStruct((T, wr), BF16),
            jax.ShapeDtypeStruct((T, wr), F32),
        ],
        scratch_shapes=[pltpu.VMEM((tm, D), BF16), pltpu.VMEM((tm, D), BF16)],
        compiler_params=pltpu.CompilerParams(
            dimension_semantics=("arbitrary",),
            vmem_limit_bytes=VMEM_LIMIT),
        name="in_proj",
    )(x2, norm_w.reshape(1, D), w_bf, cos_t, sin_t)


def _retention_kernel(lg_ref, q_ref, k_ref, v_ref, g_ref, nw_ref,
                      y_ref, kr_ref, kv_ref, r_ref, *, chunk, d):
    for hh in range(q_ref.shape[1] // d):
        head = pl.program_id(1) * (q_ref.shape[1] // d) + hh
        lanes = slice(hh * d, (hh + 1) * d)
        _retention_head(lg_ref[0, head], lg_ref[1, head], q_ref, k_ref, v_ref, g_ref, nw_ref,
                        y_ref, kr_ref, kv_ref, r_ref, lanes=lanes, chunk=chunk)


def _retention_head(lg_f, lg_b, q_ref, k_ref, v_ref, g_ref, nw_ref,
                    y_ref, kr_ref, kv_ref, r_ref, *, lanes, chunk):
    S = q_ref.shape[0]
    d = lanes.stop - lanes.start
    C = chunk
    n_chunks = S // C

    pos = lax.broadcasted_iota(jnp.int32, (C, d), 0).astype(F32)
    zeta_f = jnp.exp((C - 1 - pos) * lg_f)
    xi_f = jnp.exp((pos + 1.0) * lg_f)
    zeta_b = jnp.exp(pos * lg_b)
    xi_b = jnp.exp((C - pos) * lg_b)
    row = lax.broadcasted_iota(jnp.int32, (C, C), 0)
    col = lax.broadcasted_iota(jnp.int32, (C, C), 1)
    diff = (row - col).astype(F32)
    decay = jnp.exp(jnp.where(diff >= 0, diff * lg_f, -diff * lg_b))

    def prep(n, carry):
        r = pl.multiple_of(n * C, C)
        k = k_ref[pl.ds(r, C), lanes]
        kr_ref[pl.ds(r, C), :] = k.astype(BF16)
        kz = jnp.concatenate([k * zeta_f, k * zeta_b], axis=1).astype(BF16)
        kv_ref[n] = lax.dot_general(kz, v_ref[pl.ds(r, C), lanes], (((0,), (0,)), ((), ())),
                                    preferred_element_type=F32)
        return carry
    lax.fori_loop(0, n_chunks, prep, 0, unroll=8)

    dec_f = jnp.exp(C * lg_f)
    dec_b = jnp.exp(C * lg_b)

    def scan_f(n, state):
        r_ref[n, :, 0:d] = state.astype(BF16)
        return dec_f * state + kv_ref[n, 0:d, :]
    lax.fori_loop(0, n_chunks, scan_f, jnp.zeros((d, d), F32))

    def scan_b(t, state):
        n = n_chunks - 1 - t
        r_ref[n, :, d:2 * d] = state.astype(BF16)
        return dec_b * state + kv_ref[n, d:2 * d, :]
    lax.fori_loop(0, n_chunks, scan_b, jnp.zeros((d, d), F32))

    nw = nw_ref[:, lanes]

    def out(n, carry):
        r = pl.multiple_of(n * C, C)
        q = q_ref[pl.ds(r, C), lanes]
        s = lax.dot_general(q, kr_ref[pl.ds(r, C), :], (((1,), (1,)), ((), ())),
                            preferred_element_type=F32)
        p = (s * decay).astype(BF16)
        o = _dot(p, v_ref[pl.ds(r, C), lanes])
        cross = _dot(q, r_ref[n])
        o = o + cross[:, 0:d] * xi_f + cross[:, d:2 * d] * xi_b
        o = _rms_rows(o, nw)
        g = g_ref[pl.ds(r, C), lanes]
        y_ref[pl.ds(r, C), lanes] = (g * _sigmoid(g) * o).astype(y_ref.dtype)
        return carry
    lax.fori_loop(0, n_chunks, out, 0, unroll=8)


def _retention(q3, k3, v3, g3, lg, norm_w):
    B, S, width = q3.shape
    H = RET_HEADS
    d = width // H
    C = min(RET_CHUNK, S)
    hps = 2 if H % 2 == 0 else 1

    def head_spec():
        return pl.BlockSpec((None, S, hps * d), lambda b, h: (b, 0, h))

    return pl.pallas_call(
        functools.partial(_retention_kernel, chunk=C, d=d),
        grid=(B, H // hps),
        in_specs=[
            pl.BlockSpec(memory_space=pltpu.SMEM),
            head_spec(), head_spec(), head_spec(), head_spec(),
            pl.BlockSpec((1, hps * d), lambda b, h: (0, h)),
        ],
        out_specs=head_spec(),
        out_shape=jax.ShapeDtypeStruct((B, S, width), BF16),
        scratch_shapes=[
            pltpu.VMEM((S, d), BF16),
            pltpu.VMEM((S // C, 2 * d, d), F32),
            pltpu.VMEM((S // C, d, 2 * d), BF16),
        ],
        compiler_params=pltpu.CompilerParams(
            dimension_semantics=("parallel", "arbitrary"),
            vmem_limit_bytes=VMEM_LIMIT),
        name="retention",
    )(lg, q3, k3, v3, g3, norm_w.reshape(1, width))


def _mix_out_kernel(x_ref, hg_ref, hp_ref, hn_ref, yr_ref, w_ref, cw_ref, cb_ref, lw_ref, lb_ref,
                    n2_ref, o_ref, h2_ref, xs_ref, cv_ref,
                    *, parts, conv_rows, norm_block, row_chunk):
    tm, W = hg_ref.shape
    n_lt = W // LANES
    i = pl.program_id(1)
    last = pl.num_programs(1) - 1
    off0 = HALO - CONV_PAD
    part_rows = tm // parts
    reps = conv_rows // SUBLANES

    for t in range(n_lt):
        ls = slice(t * LANES, (t + 1) * LANES)
        xs_ref[t, HALO:HALO + tm, :] = hg_ref[:, ls]
        xs_ref[t, 0:HALO, :] = jnp.where(i > 0, hp_ref[:, ls], 0.0)
        xs_ref[t, HALO + tm:2 * HALO + tm, :] = jnp.where(i < last, hn_ref[:, ls], 0.0)

    for p in range(parts):
        r0 = p * part_rows
        for rc in range(r0, r0 + part_rows, conv_rows):
            for t in range(n_lt):
                acc = jnp.concatenate([cb_ref[t]] * reps, axis=0)
                for k in range(CONV_KERNEL):
                    wk = jnp.concatenate([cw_ref[t, k]] * reps, axis=0)
                    acc = acc + xs_ref[t, rc + off0 + k:rc + off0 + k + conv_rows, :] * wk
                cv_ref[t, rc:rc + conv_rows, :] = acc
        ycs = []
        for nb in range(part_rows // norm_block):
            rr = slice(r0 + nb * norm_block, r0 + (nb + 1) * norm_block)
            c = jnp.concatenate([cv_ref[t, rr, :] for t in range(n_lt)], axis=1)
            mu = jnp.mean(c, axis=-1, keepdims=True)
            cc = c - mu
            var = jnp.mean(cc * cc, axis=-1, keepdims=True)
            z = cc * lax.rsqrt(var + EPS) * lw_ref[...] + lb_ref[...]
            ycs.append((z * _sigmoid(z)).astype(BF16))
        yc = jnp.concatenate(ycs, axis=0)
        acc = _dot(yc, w_ref[0:W, :]) + _dot(yr_ref[r0:r0 + part_rows, :], w_ref[W:, :])
        for c in range(part_rows // row_chunk):
            sub = slice(r0 + c * row_chunk, r0 + (c + 1) * row_chunk)
            x1 = x_ref[sub, :] + acc[c * row_chunk:(c + 1) * row_chunk, :]
            o_ref[sub, :] = x1
            h2_ref[sub, :] = _rms_rows(x1, n2_ref[...]).astype(h2_ref.dtype)


def _mix_out(x3, hg3, yr3, w_bf, conv_w, conv_b, ln_w, ln_b, norm2_w, *, tm=512):
    B, S, D = x3.shape
    W = hg3.shape[-1]
    wr = yr3.shape[-1]
    n_lt = W // LANES
    tm = min(tm, S)
    hb = tm // HALO
    n_halo = S // HALO
    cw8 = jnp.broadcast_to(conv_w.reshape(CONV_KERNEL, n_lt, 1, LANES).transpose(1, 0, 2, 3),
                           (n_lt, CONV_KERNEL, SUBLANES, LANES))
    cb8 = jnp.broadcast_to(conv_b.reshape(n_lt, 1, LANES), (n_lt, SUBLANES, LANES))

    def rows(width):
        return pl.BlockSpec((None, tm, width), lambda b, i: (b, i, 0))

    def const(shape):
        return pl.BlockSpec(shape, lambda b, i: (0,) * len(shape))

    return pl.pallas_call(
        functools.partial(_mix_out_kernel, parts=max(tm // 256, 1), conv_rows=min(256, tm),
                          norm_block=32, row_chunk=64),
        grid=(B, S // tm),
        in_specs=[
            rows(D),
            rows(W),
            pl.BlockSpec((None, HALO, W), lambda b, i: (b, jnp.maximum(i * hb - 1, 0), 0)),
            pl.BlockSpec((None, HALO, W), lambda b, i: (b, jnp.minimum((i + 1) * hb, n_halo - 1), 0)),
            rows(wr),
            const((W + wr, D)),
            const((n_lt, CONV_KERNEL, SUBLANES, LANES)),
            const((n_lt, SUBLANES, LANES)),
            const((1, W)), const((1, W)), const((1, D)),
        ],
        out_specs=[rows(D), rows(D)],
        out_shape=[jax.ShapeDtypeStruct((B, S, D), F32), jax.ShapeDtypeStruct((B, S, D), BF16)],
        scratch_shapes=[pltpu.VMEM((n_lt, tm + 2 * HALO, LANES), F32),
                        pltpu.VMEM((n_lt, tm, LANES), F32)],
        compiler_params=pltpu.CompilerParams(
            dimension_semantics=("parallel", "arbitrary"),
            vmem_limit_bytes=VMEM_LIMIT),
        name="mix_out",
    )(x3, hg3, hg3, hg3, yr3, w_bf, cw8, cb8, ln_w.reshape(1, W), ln_b.reshape(1, W),
      norm2_w.reshape(1, D))


def _ffn_kernel(h_ref, x_hbm, wg_ref, wu_ref, wo_ref, fn_ref, o_ref, sem,
                *, row_chunk, final_norm):
    tm = h_ref.shape[0]
    tf = wg_ref.shape[1]
    half = tf // 2
    i = pl.program_id(0)
    j = pl.program_id(1)

    def residual_copy():
        return pltpu.make_async_copy(x_hbm.at[pl.ds(i * tm, tm), :], o_ref, sem)

    def hidden():
        h = h_ref[...]
        g0 = _dot(h, wg_ref[:, 0:half])
        u0 = _dot(h, wu_ref[:, 0:half])
        g1 = _dot(h, wg_ref[:, half:tf])
        u1 = _dot(h, wu_ref[:, half:tf])
        a0 = (g0 * _sigmoid(g0) * u0).astype(BF16)
        a1 = (g1 * _sigmoid(g1) * u1).astype(BF16)
        return a0, a1

    def accumulate(a0, a1):
        o_ref[...] += _dot(a0, wo_ref[0:half, :]) + _dot(a1, wo_ref[half:tf, :])

    @pl.when(j == 0)
    def _():
        residual_copy().start()
        a0, a1 = hidden()
        residual_copy().wait()
        accumulate(a0, a1)

    @pl.when(j > 0)
    def _():
        accumulate(*hidden())

    if final_norm:
        @pl.when(j == pl.num_programs(1) - 1)
        def _():
            for c in range(tm // row_chunk):
                rows = slice(c * row_chunk, (c + 1) * row_chunk)
                o_ref[rows, :] = _rms_rows(o_ref[rows, :], fn_ref[...])


def _ffn(h2, x1, w_in_bf, w_out_bf, final_w, *, final_norm, tm=1024, tf=512):
    T, D = x1.shape
    F = w_out_bf.shape[0]
    tm = min(tm, T)
    nf = F // tf
    assert F % tf == 0 and T % tm == 0
    return pl.pallas_call(
        functools.partial(_ffn_kernel, row_chunk=64, final_norm=final_norm),
        grid=(T // tm, nf),
        in_specs=[
            pl.BlockSpec((tm, D), lambda i, j: (i, 0)),
            pl.BlockSpec(memory_space=pl.ANY),
            pl.BlockSpec((D, tf), lambda i, j: (0, j)),
            pl.BlockSpec((D, tf), lambda i, j: (0, j + nf)),
            pl.BlockSpec((tf, D), lambda i, j: (j, 0)),
            pl.BlockSpec((1, D), lambda i, j: (0, 0)),
        ],
        out_specs=pl.BlockSpec((tm, D), lambda i, j: (i, 0)),
        out_shape=jax.ShapeDtypeStruct((T, D), F32),
        scratch_shapes=[pltpu.SemaphoreType.DMA],
        compiler_params=pltpu.CompilerParams(
            dimension_semantics=("parallel", "arbitrary"),
            vmem_limit_bytes=VMEM_LIMIT),
        name="ffn",
    )(h2, x1, w_in_bf, w_in_bf, w_out_bf, final_w.reshape(1, D))


def _rope_tables(S, d):
    inv_freq = ROPE_BASE ** (-jnp.arange(0, d, 2, dtype=F32) / d)
    ang = jnp.arange(S, dtype=F32)[:, None] * inv_freq[None, :]
    cos, sin = jnp.cos(ang), jnp.sin(ang)
    return jnp.concatenate([cos, cos], axis=1), jnp.concatenate([-sin, sin], axis=1)


def _trunk(x, p):
    B, S, D = x.shape
    depth = p["w_in_bf"].shape[0]
    wc = p["conv_w"].shape[-1]
    wr = p["ret_norm_w"].shape[-1]
    cos_t, sin_t = _rope_tables(S, wr // RET_HEADS)
    x2 = x.reshape(B * S, D)
    for l in range(depth):
        last = l == depth - 1
        hg, q, k, v, g = _in_proj(x2, p["norm1_w"][l], p["w_in_bf"][l], cos_t, sin_t,
                                  seq=S, wc=wc, wr=wr)
        lg = jnp.stack([p["ret_log_decay_fwd"][l], p["ret_log_decay_bwd"][l]]).astype(F32)
        yr = _retention(q.reshape(B, S, wr), k.reshape(B, S, wr), v.reshape(B, S, wr),
                        g.reshape(B, S, wr), lg, p["ret_norm_w"][l])
        x1, h2 = _mix_out(x2.reshape(B, S, D), hg.reshape(B, S, wc), yr, p["w_out_bf"][l],
                          p["conv_w"][l], p["conv_b"][l], p["conv_ln_w"][l], p["conv_ln_b"][l],
                          p["norm2_w"][l])
        x2 = _ffn(h2.reshape(B * S, D), x1.reshape(B * S, D), p["w_ffn_in_bf"][l],
                  p["w_ffn_out_bf"][l], p["final_norm_w"], final_norm=last)
    return x2.reshape(B, S, D)


def kernel(x_prompt, x_sample, norm1_w, w_in, conv_w, conv_b, conv_ln_w, conv_ln_b,
           ret_log_decay_fwd, ret_log_decay_bwd, ret_norm_w, w_out,
           norm2_w, w_ffn_in, w_ffn_out, final_norm_w):
    p = dict(norm1_w=norm1_w, conv_w=conv_w, conv_b=conv_b, conv_ln_w=conv_ln_w,
             conv_ln_b=conv_ln_b, ret_log_decay_fwd=ret_log_decay_fwd,
             ret_log_decay_bwd=ret_log_decay_bwd, ret_norm_w=ret_norm_w,
             norm2_w=norm2_w, final_norm_w=final_norm_w,
             w_in_bf=w_in.astype(BF16), w_out_bf=w_out.astype(BF16),
             w_ffn_in_bf=w_ffn_in.astype(BF16), w_ffn_out_bf=w_ffn_out.astype(BF16))
    return _trunk(x_prompt, p), _trunk(x_sample, p)
```

```python
import functools

import jax
import jax.numpy as jnp
from jax import lax
from jax.experimental import pallas as pl
from jax.experimental.pallas import tpu as pltpu

F32 = jnp.float32
BF16 = jnp.bfloat16

EPS = 1e-6
ROPE_BASE = 10000.0
RET_HEADS = 8
CONV_KERNEL = 31
CONV_PAD = CONV_KERNEL // 2
SUBLANES = 8
LANES = 128
HALO = 16
RET_CHUNK = 256
VMEM_LIMIT = 56 * 1024 * 1024
RET_WINDOW_BYTES = 32 * 1024 * 1024


def _sigmoid(x):
    return 1.0 / (1.0 + jnp.exp(-x))


def _rms_rows(x, w):
    ms = jnp.mean(x * x, axis=-1, keepdims=True)
    return x * lax.rsqrt(ms + EPS) * w


def _dot(a, b):
    return jnp.dot(a, b, preferred_element_type=F32)


def _in_proj_kernel(x_ref, nw_ref, w_ref, cos_ref, sin_ref,
                    hg_ref, q_ref, k_ref, v_ref, g_ref, h0_ref, h1_ref,
                    *, wc, wr, heads, row_chunk):
    s = pl.program_id(0)
    tm = x_ref.shape[0]
    d = wr // heads

    def normalise(h_write):
        for c in range(tm // row_chunk):
            rows = slice(c * row_chunk, (c + 1) * row_chunk)
            h_write[rows, :] = _rms_rows(x_ref[rows, :], nw_ref[...]).astype(BF16)

    def step(h_read, h_write):
        h = h_read[...]

        def proj(c0, width):
            return _dot(h, w_ref[:, c0:c0 + width])

        hg_ref[...] = proj(0, wc) * _sigmoid(proj(wc, wc))
        cos = cos_ref[...]
        sin = sin_ref[...]

        def rotary(x, hd):
            xh = x[:, hd * d:(hd + 1) * d]
            return xh * cos + pltpu.roll(xh, d // 2, axis=1) * sin

        q = proj(2 * wc, wr)
        for hd in range(heads):
            q_ref[:, hd * d:(hd + 1) * d] = rotary(q, hd).astype(q_ref.dtype)
        k = proj(2 * wc + wr, wr)
        for hd in range(heads):
            k_ref[:, hd * d:(hd + 1) * d] = rotary(k, hd) * (d ** -0.5)
        v_ref[...] = proj(2 * wc + 2 * wr, wr).astype(v_ref.dtype)
        g_ref[...] = proj(2 * wc + 3 * wr, wr)
        normalise(h_write)

    @pl.when(s == 0)
    def _():
        normalise(h0_ref)

    @pl.when((s > 0) & (s % 2 == 0))
    def _():
        step(h1_ref, h0_ref)

    @pl.when(s % 2 == 1)
    def _():
        step(h0_ref, h1_ref)


def _in_proj(x2, norm_w, w_bf, cos_t, sin_t, *, seq, wc, wr, tm=256):
    T, D = x2.shape
    N = w_bf.shape[1]
    tm = min(tm, seq)
    n = T // tm
    d = wr // RET_HEADS
    per_seq = seq // tm

    def row_spec(width):
        return pl.BlockSpec((tm, width), lambda s: (jnp.maximum(s - 1, 0), 0))

    return pl.pallas_call(
        functools.partial(_in_proj_kernel, wc=wc, wr=wr, heads=RET_HEADS, row_chunk=64),
        grid=(n + 1,),
        in_specs=[
            pl.BlockSpec((tm, D), lambda s: (jnp.minimum(s, n - 1), 0)),
            pl.BlockSpec((1, D), lambda s: (0, 0)),
            pl.BlockSpec((D, N), lambda s: (0, 0), pipeline_mode=pl.Buffered(1)),
            pl.BlockSpec((tm, d), lambda s: (jnp.maximum(s - 1, 0) % per_seq, 0)),
            pl.BlockSpec((tm, d), lambda s: (jnp.maximum(s - 1, 0) % per_seq, 0)),
        ],
        out_specs=[row_spec(wc), row_spec(wr), row_spec(wr), row_spec(wr), row_spec(wr)],
        out_shape=[
            jax.ShapeDtypeStruct((T, wc), F32),
            jax.ShapeDtypeStruct((T, wr), BF16),
            jax.ShapeDtypeStruct((T, wr), F32),
            jax.ShapeDtypeStruct((T, wr), BF16),
            jax.ShapeDtypeStruct((T, wr), F32),
        ],
        scratch_shapes=[pltpu.VMEM((tm, D), BF16), pltpu.VMEM((tm, D), BF16)],
        compiler_params=pltpu.CompilerParams(
            dimension_semantics=("arbitrary",),
            vmem_limit_bytes=VMEM_LIMIT),
        name="in_proj",
    )(x2, norm_w.reshape(1, D), w_bf, cos_t, sin_t)


def _retention_kernel(lg_ref, q_ref, k_ref, v_ref, g_ref, nw_ref,
                      y_ref, kr_ref, kv_ref, r_ref, *, chunk, d):
    for hh in range(q_ref.shape[1] // d):
        head = pl.program_id(1) * (q_ref.shape[1] // d) + hh
        lanes = slice(hh * d, (hh + 1) * d)
        _retention_head(lg_ref[0, head], lg_ref[1, head], q_ref, k_ref, v_ref, g_ref, nw_ref,
                        y_ref, kr_ref, kv_ref, r_ref, lanes=lanes, chunk=chunk)


def _retention_head(lg_f, lg_b, q_ref, k_ref, v_ref, g_ref, nw_ref,
                    y_ref, kr_ref, kv_ref, r_ref, *, lanes, chunk):
    S = q_ref.shape[0]
    d = lanes.stop - lanes.start
    C = chunk
    n_chunks = S // C

    pos = lax.broadcasted_iota(jnp.int32, (C, d), 0).astype(F32)
    zeta_f = jnp.exp((C - 1 - pos) * lg_f)
    xi_f = jnp.exp((pos + 1.0) * lg_f)
    zeta_b = jnp.exp(pos * lg_b)
    xi_b = jnp.exp((C - pos) * lg_b)
    row = lax.broadcasted_iota(jnp.int32, (C, C), 0)
    col = lax.broadcasted_iota(jnp.int32, (C, C), 1)
    diff = (row - col).astype(F32)
    decay = jnp.exp(jnp.where(diff >= 0, diff * lg_f, -diff * lg_b))

    def prep(n, carry):
        r = pl.multiple_of(n * C, C)
        k = k_ref[pl.ds(r, C), lanes]
        kr_ref[pl.ds(r, C), :] = k.astype(BF16)
        kz = jnp.concatenate([k * zeta_f, k * zeta_b], axis=1).astype(BF16)
        kv_ref[n] = lax.dot_general(kz, v_ref[pl.ds(r, C), lanes], (((0,), (0,)), ((), ())),
                                    preferred_element_type=F32)
        return carry
    lax.fori_loop(0, n_chunks, prep, 0, unroll=8)

    dec_f = jnp.exp(C * lg_f)
    dec_b = jnp.exp(C * lg_b)

    def scan_f(n, state):
        r_ref[n, :, 0:d] = state.astype(BF16)
        return dec_f * state + kv_ref[n, 0:d, :]
    lax.fori_loop(0, n_chunks, scan_f, jnp.zeros((d, d), F32))

    def scan_b(t, state):
        n = n_chunks - 1 - t
        r_ref[n, :, d:2 * d] = state.astype(BF16)
        return dec_b * state + kv_ref[n, d:2 * d, :]
    lax.fori_loop(0, n_chunks, scan_b, jnp.zeros((d, d), F32))

    nw = nw_ref[:, lanes]

    def out(n, carry):
        r = pl.multiple_of(n * C, C)
        q = q_ref[pl.ds(r, C), lanes]
        s = lax.dot_general(q, kr_ref[pl.ds(r, C), :], (((1,), (1,)), ((), ())),
                            preferred_element_type=F32)
        p = (s * decay).astype(BF16)
        o = _dot(p, v_ref[pl.ds(r, C), lanes])
        cross = _dot(q, r_ref[n])
        o = o + cross[:, 0:d] * xi_f + cross[:, d:2 * d] * xi_b
        o = _rms_rows(o, nw)
        g = g_ref[pl.ds(r, C), lanes]
        y_ref[pl.ds(r, C), lanes] = (g * _sigmoid(g) * o).astype(y_ref.dtype)
        return carry
    lax.fori_loop(0, n_chunks, out, 0, unroll=8)


def _retention(q3, k3, v3, g3, lg, norm_w):
    B, S, width = q3.shape
    H = RET_HEADS
    d = width // H
    C = min(RET_CHUNK, S)
    hps = 1
    while H % (2 * hps) == 0 and 2 * 14 * S * d * (2 * hps) <= RET_WINDOW_BYTES:
        hps *= 2

    def head_spec():
        return pl.BlockSpec((None, S, hps * d), lambda b, h: (b, 0, h))

    return pl.pallas_call(
        functools.partial(_retention_kernel, chunk=C, d=d),
        grid=(B, H // hps),
        in_specs=[
            pl.BlockSpec(memory_space=pltpu.SMEM),
            head_spec(), head_spec(), head_spec(), head_spec(),
            pl.BlockSpec((1, hps * d), lambda b, h: (0, h)),
        ],
        out_specs=head_spec(),
        out_shape=jax.ShapeDtypeStruct((B, S, width), BF16),
        scratch_shapes=[
            pltpu.VMEM((S, d), BF16),
            pltpu.VMEM((S // C, 2 * d, d), F32),
            pltpu.VMEM((S // C, d, 2 * d), BF16),
        ],
        compiler_params=pltpu.CompilerParams(
            dimension_semantics=("parallel", "arbitrary"),
            vmem_limit_bytes=VMEM_LIMIT),
        name="retention",
    )(lg, q3, k3, v3, g3, norm_w.reshape(1, width))


def _mix_out_kernel(x_ref, hg_ref, hp_ref, hn_ref, yr_ref, w_ref, cw_ref, cb_ref, lw_ref, lb_ref,
                    n2_ref, o_ref, h2_ref, xs_ref, cv_ref,
                    *, parts, conv_rows, norm_block, row_chunk):
    tm, W = hg_ref.shape
    n_lt = W // LANES
    i = pl.program_id(1)
    last = pl.num_programs(1) - 1
    off0 = HALO - CONV_PAD
    part_rows = tm // parts
    reps = conv_rows // SUBLANES

    for t in range(n_lt):
        ls = slice(t * LANES, (t + 1) * LANES)
        xs_ref[t, HALO:HALO + tm, :] = hg_ref[:, ls]
        xs_ref[t, 0:HALO, :] = jnp.where(i > 0, hp_ref[:, ls], 0.0)
        xs_ref[t, HALO + tm:2 * HALO + tm, :] = jnp.where(i < last, hn_ref[:, ls], 0.0)

    for p in range(parts):
        r0 = p * part_rows
        for rc in range(r0, r0 + part_rows, conv_rows):
            for t in range(n_lt):
                acc = jnp.concatenate([cb_ref[t]] * reps, axis=0)
                for k in range(CONV_KERNEL):
                    wk = jnp.concatenate([cw_ref[t, k]] * reps, axis=0)
                    acc = acc + xs_ref[t, rc + off0 + k:rc + off0 + k + conv_rows, :] * wk
                cv_ref[t, rc:rc + conv_rows, :] = acc
        ycs = []
        for nb in range(part_rows // norm_block):
            rr = slice(r0 + nb * norm_block, r0 + (nb + 1) * norm_block)
            c = jnp.concatenate([cv_ref[t, rr, :] for t in range(n_lt)], axis=1)
            mu = jnp.mean(c, axis=-1, keepdims=True)
            cc = c - mu
            var = jnp.mean(cc * cc, axis=-1, keepdims=True)
            z = cc * lax.rsqrt(var + EPS) * lw_ref[...] + lb_ref[...]
            ycs.append((z * _sigmoid(z)).astype(BF16))
        yc = jnp.concatenate(ycs, axis=0)
        acc = _dot(yc, w_ref[0:W, :]) + _dot(yr_ref[r0:r0 + part_rows, :], w_ref[W:, :])
        for c in range(part_rows // row_chunk):
            sub = slice(r0 + c * row_chunk, r0 + (c + 1) * row_chunk)
            x1 = x_ref[sub, :] + acc[c * row_chunk:(c + 1) * row_chunk, :]
            o_ref[sub, :] = x1
            h2_ref[sub, :] = _rms_rows(x1, n2_ref[...]).astype(h2_ref.dtype)


def _mix_out(x3, hg3, yr3, w_bf, conv_w, conv_b, ln_w, ln_b, norm2_w, *, tm=512):
    B, S, D = x3.shape
    W = hg3.shape[-1]
    wr = yr3.shape[-1]
    n_lt = W // LANES
    tm = min(tm, S)
    hb = tm // HALO
    n_halo = S // HALO
    cw8 = jnp.broadcast_to(conv_w.reshape(CONV_KERNEL, n_lt, 1, LANES).transpose(1, 0, 2, 3),
                           (n_lt, CONV_KERNEL, SUBLANES, LANES))
    cb8 = jnp.broadcast_to(conv_b.reshape(n_lt, 1, LANES), (n_lt, SUBLANES, LANES))

    def rows(width):
        return pl.BlockSpec((None, tm, width), lambda b, i: (b, i, 0))

    def const(shape):
        return pl.BlockSpec(shape, lambda b, i: (0,) * len(shape))

    return pl.pallas_call(
        functools.partial(_mix_out_kernel, parts=max(tm // 256, 1), conv_rows=min(256, tm),
                          norm_block=32, row_chunk=64),
        grid=(B, S // tm),
        in_specs=[
            rows(D),
            rows(W),
            pl.BlockSpec((None, HALO, W), lambda b, i: (b, jnp.maximum(i * hb - 1, 0), 0)),
            pl.BlockSpec((None, HALO, W), lambda b, i: (b, jnp.minimum((i + 1) * hb, n_halo - 1), 0)),
            rows(wr),
            const((W + wr, D)),
            const((n_lt, CONV_KERNEL, SUBLANES, LANES)),
            const((n_lt, SUBLANES, LANES)),
            const((1, W)), const((1, W)), const((1, D)),
        ],
        out_specs=[rows(D), rows(D)],
        out_shape=[jax.ShapeDtypeStruct((B, S, D), F32), jax.ShapeDtypeStruct((B, S, D), BF16)],
        scratch_shapes=[pltpu.VMEM((n_lt, tm + 2 * HALO, LANES), F32),
                        pltpu.VMEM((n_lt, tm, LANES), F32)],
        compiler_params=pltpu.CompilerParams(
            dimension_semantics=("parallel", "arbitrary"),
            vmem_limit_bytes=VMEM_LIMIT),
        name="mix_out",
    )(x3, hg3, hg3, hg3, yr3, w_bf, cw8, cb8, ln_w.reshape(1, W), ln_b.reshape(1, W),
      norm2_w.reshape(1, D))


def _ffn_kernel(h_ref, x_hbm, wg_ref, wu_ref, wo_ref, fn_ref, o_ref, sem,
                *, row_chunk, final_norm):
    tm = h_ref.shape[0]
    tf = wg_ref.shape[1]
    half = tf // 2
    i = pl.program_id(0)
    j = pl.program_id(1)

    def residual_copy():
        return pltpu.make_async_copy(x_hbm.at[pl.ds(i * tm, tm), :], o_ref, sem)

    def hidden():
        h = h_ref[...]
        g0 = _dot(h, wg_ref[:, 0:half])
        u0 = _dot(h, wu_ref[:, 0:half])
        g1 = _dot(h, wg_ref[:, half:tf])
        u1 = _dot(h, wu_ref[:, half:tf])
        a0 = (g0 * _sigmoid(g0) * u0).astype(BF16)
        a1 = (g1 * _sigmoid(g1) * u1).astype(BF16)
        return a0, a1

    def accumulate(a0, a1):
        o_ref[...] += _dot(a0, wo_ref[0:half, :]) + _dot(a1, wo_ref[half:tf, :])

    @pl.when(j == 0)
    def _():
        residual_copy().start()
        a0, a1 = hidden()
        residual_copy().wait()
        accumulate(a0, a1)

    @pl.when(j > 0)
    def _():
        accumulate(*hidden())

    if final_norm:
        @pl.when(j == pl.num_programs(1) - 1)
        def _():
            for c in range(tm // row_chunk):
                rows = slice(c * row_chunk, (c + 1) * row_chunk)
                o_ref[rows, :] = _rms_rows(o_ref[rows, :], fn_ref[...])


def _ffn(h2, x1, w_in_bf, w_out_bf, final_w, *, final_norm, tm=1024, tf=512):
    T, D = x1.shape
    F = w_out_bf.shape[0]
    tm = min(tm, T)
    nf = F // tf
    assert F % tf == 0 and T % tm == 0
    return pl.pallas_call(
        functools.partial(_ffn_kernel, row_chunk=64, final_norm=final_norm),
        grid=(T // tm, nf),
        in_specs=[
            pl.BlockSpec((tm, D), lambda i, j: (i, 0)),
            pl.BlockSpec(memory_space=pl.ANY),
            pl.BlockSpec((D, tf), lambda i, j: (0, j)),
            pl.BlockSpec((D, tf), lambda i, j: (0, j + nf)),
            pl.BlockSpec((tf, D), lambda i, j: (j, 0)),
            pl.BlockSpec((1, D), lambda i, j: (0, 0)),
        ],
        out_specs=pl.BlockSpec((tm, D), lambda i, j: (i, 0)),
        out_shape=jax.ShapeDtypeStruct((T, D), F32),
        scratch_shapes=[pltpu.SemaphoreType.DMA],
        compiler_params=pltpu.CompilerParams(
            dimension_semantics=("parallel", "arbitrary"),
            vmem_limit_bytes=VMEM_LIMIT),
        name="ffn",
    )(h2, x1, w_in_bf, w_in_bf, w_out_bf, final_w.reshape(1, D))


def _rope_tables(S, d):
    inv_freq = ROPE_BASE ** (-jnp.arange(0, d, 2, dtype=F32) / d)
    ang = jnp.arange(S, dtype=F32)[:, None] * inv_freq[None, :]
    cos, sin = jnp.cos(ang), jnp.sin(ang)
    return jnp.concatenate([cos, cos], axis=1), jnp.concatenate([-sin, sin], axis=1)


def _trunk(x, p):
    B, S, D = x.shape
    depth = p["w_in_bf"].shape[0]
    wc = p["conv_w"].shape[-1]
    wr = p["ret_norm_w"].shape[-1]
    cos_t, sin_t = _rope_tables(S, wr // RET_HEADS)
    x2 = x.reshape(B * S, D)
    for l in range(depth):
        last = l == depth - 1
        hg, q, k, v, g = _in_proj(x2, p["norm1_w"][l], p["w_in_bf"][l], cos_t, sin_t,
                                  seq=S, wc=wc, wr=wr)
        lg = jnp.stack([p["ret_log_decay_fwd"][l], p["ret_log_decay_bwd"][l]]).astype(F32)
        yr = _retention(q.reshape(B, S, wr), k.reshape(B, S, wr), v.reshape(B, S, wr),
                        g.reshape(B, S, wr), lg, p["ret_norm_w"][l])
        x1, h2 = _mix_out(x2.reshape(B, S, D), hg.reshape(B, S, wc), yr, p["w_out_bf"][l],
                          p["conv_w"][l], p["conv_b"][l], p["conv_ln_w"][l], p["conv_ln_b"][l],
                          p["norm2_w"][l])
        x2 = _ffn(h2.reshape(B * S, D), x1.reshape(B * S, D), p["w_ffn_in_bf"][l],
                  p["w_ffn_out_bf"][l], p["final_norm_w"], final_norm=last)
    return x2.reshape(B, S, D)


def kernel(x_prompt, x_sample, norm1_w, w_in, conv_w, conv_b, conv_ln_w, conv_ln_b,
           ret_log_decay_fwd, ret_log_decay_bwd, ret_norm_w, w_out,
           norm2_w, w_ffn_in, w_ffn_out, final_norm_w):
    p = dict(norm1_w=norm1_w, conv_w=conv_w, conv_b=conv_b, conv_ln_w=conv_ln_w,
             conv_ln_b=conv_ln_b, ret_log_decay_fwd=ret_log_decay_fwd,
             ret_log_decay_bwd=ret_log_decay_bwd, ret_norm_w=ret_norm_w,
             norm2_w=norm2_w, final_norm_w=final_norm_w,
             w_in_bf=w_in.astype(BF16), w_out_bf=w_out.astype(BF16),
             w_ffn_in_bf=w_ffn_in.astype(BF16), w_ffn_out_bf=w_ffn_out.astype(BF16))
    return _trunk(x_prompt, p), _trunk(x_sample, p)
```
